```python
import math
import jax, jax.numpy as jnp
from jax import lax
import numpy as np

D_MODEL = 2048
BATCH = 2
SEQ = 4096
DEPTH = 4
DEC_BATCH = 2
DEC_SEQ = 16384
PAST_LEN = 128

N_META = 16
ATTN_WIDTH = D_MODEL // 2
SSM_WIDTH = D_MODEL - ATTN_WIDTH
HEAD_DIM = 64
N_HEADS = ATTN_WIDTH // HEAD_DIM
N_KV_HEADS = N_HEADS // 4
KV_REP = N_HEADS // N_KV_HEADS
KV_WIDTH = N_KV_HEADS * HEAD_DIM
ROT_DIM = HEAD_DIM // 4
ROPE_THETA = 500000.0
WINDOW = 128
BLOCK = 128
SSM_GROUP = 16
SSM_GROUPS = SSM_WIDTH // SSM_GROUP
SSM_STATE = 64
D_FF = 4 * D_MODEL
IN_WIDTH = ATTN_WIDTH + 2 * KV_WIDTH + SSM_WIDTH
NORM_EPS = 1e-6
NEG_INF = -1e30

kernel_name = "hymba_style_bidir_attn_s5_encoder"

F32 = jnp.float32


def rmsnorm(x, g):
    xf = x.astype(F32)
    y = xf * lax.rsqrt(jnp.mean(xf * xf, axis=-1, keepdims=True) + NORM_EPS)
    return (y * g.astype(F32)).astype(x.dtype)


def partial_rope(x, pos):
    half = ROT_DIM // 2
    inv = ROPE_THETA ** (-jnp.arange(half, dtype=F32) * 2.0 / ROT_DIM)
    ang = pos[:, None] * inv[None, :]
    cos = jnp.cos(ang)[None, :, None, :].astype(x.dtype)
    sin = jnp.sin(ang)[None, :, None, :].astype(x.dtype)
    x1 = x[..., :half]
    x2 = x[..., half:ROT_DIM]
    return jnp.concatenate([x1 * cos - x2 * sin, x2 * cos + x1 * sin, x[..., ROT_DIM:]], axis=-1)


def sink_softmax(s, sink):
    m = jnp.maximum(jnp.max(s, axis=-1, keepdims=True), sink)
    e = jnp.exp(s - m)
    return e / (jnp.sum(e, axis=-1, keepdims=True) + jnp.exp(sink - m))


def windowed_attention(q, k, v, sink):
    B, Lt = q.shape[0], q.shape[1]
    S = Lt - N_META
    nb = S // BLOCK
    scale = HEAD_DIM ** -0.5
    q = q.reshape(B, Lt, N_KV_HEADS, KV_REP, HEAD_DIM)
    sink = sink.astype(F32).reshape(N_KV_HEADS, KV_REP)[:, :, None, None]
    qm, qr = q[:, :N_META], q[:, N_META:]
    km, kr = k[:, :N_META], k[:, N_META:]
    vm, vr = v[:, :N_META], v[:, N_META:]

    qb = qr.reshape(B, nb, BLOCK, N_KV_HEADS, KV_REP, HEAD_DIM)

    def band(t):
        tp = jnp.pad(t, ((0, 0), (BLOCK, BLOCK), (0, 0), (0, 0)))
        tp = tp.reshape(B, nb + 2, BLOCK, N_KV_HEADS, HEAD_DIM)
        return jnp.concatenate([tp[:, :-2], tp[:, 1:-1], tp[:, 2:]], axis=2)

    kw, vw = band(kr), band(vr)
    s_loc = jnp.einsum('bnqgrd,bnkgd->bngrqk', qb, kw, preferred_element_type=F32) * scale
    s_met = jnp.einsum('bnqgrd,bmgd->bngrqm', qb, km, preferred_element_type=F32) * scale
    blk = jnp.arange(nb)[:, None, None]
    qi = blk * BLOCK + jnp.arange(BLOCK)[None, :, None]
    ki = (blk - 1) * BLOCK + jnp.arange(3 * BLOCK)[None, None, :]
    valid = (ki >= 0) & (ki < S) & (jnp.abs(qi - ki) <= WINDOW)
    s_loc = jnp.where(valid[None, :, None, None], s_loc, NEG_INF)
    p = sink_softmax(jnp.concatenate([s_met, s_loc], axis=-1), sink).astype(v.dtype)
    o = (jnp.einsum('bngrqm,bmgd->bnqgrd', p[..., :N_META], vm)
         + jnp.einsum('bngrqk,bnkgd->bnqgrd', p[..., N_META:], vw))
    o_real = o.reshape(B, S, N_HEADS * HEAD_DIM)

    kh = jnp.concatenate([km, kr[:, :BLOCK]], axis=1)
    vh = jnp.concatenate([vm, vr[:, :BLOCK]], axis=1)
    s_h = jnp.einsum('bqgrd,bkgd->bgrqk', qm, kh, preferred_element_type=F32) * scale
    valid_h = jnp.abs(jnp.arange(N_META)[:, None] - jnp.arange(N_META + BLOCK)[None, :]) <= WINDOW
    s_h = jnp.where(valid_h, s_h, NEG_INF)
    p_h = sink_softmax(s_h, sink).astype(v.dtype)
    o_meta = jnp.einsum('bgrqk,bkgd->bqgrd', p_h, vh).reshape(B, N_META, N_HEADS * HEAD_DIM)
    return jnp.concatenate([o_meta, o_real], axis=1)


def s5_discretize(a_re, a_im, log_dt, b_re, b_im):
    a_re = a_re.astype(F32)
    a_im = a_im.astype(F32)
    b_re = b_re.astype(F32)
    b_im = b_im.astype(F32)
    dt = jnp.exp(log_dt.astype(F32))[..., None]
    mag = jnp.exp(a_re * dt)
    ang = a_im * dt
    abr = mag * jnp.cos(ang)
    abi = mag * jnp.sin(ang)
    den = a_re * a_re + a_im * a_im
    nr = abr - 1.0
    cr = (nr * a_re + abi * a_im) / den
    ci = (abi * a_re - nr * a_im) / den
    bbr = cr[..., None] * b_re - ci[..., None] * b_im
    bbi = cr[..., None] * b_im + ci[..., None] * b_re
    return abr, abi, bbr, bbi


def complex_linear_combine(e1, e2):
    ar1, ai1, br1, bi1 = e1
    ar2, ai2, br2, bi2 = e2
    return (ar1 * ar2 - ai1 * ai2,
            ar1 * ai2 + ai1 * ar2,
            ar2 * br1 - ai2 * bi1 + br2,
            ar2 * bi1 + ai2 * br1 + bi2)


def s5_scan_dir(ug, abr, abi, bbr, bbi, c_re, c_im, reverse):
    bu_r = jnp.einsum('lgh,gph->lgp', ug, bbr)
    bu_i = jnp.einsum('lgh,gph->lgp', ug, bbi)
    ar = jnp.broadcast_to(abr[None], bu_r.shape)
    ai = jnp.broadcast_to(abi[None], bu_r.shape)
    _, _, hr, hi = lax.associative_scan(complex_linear_combine, (ar, ai, bu_r, bu_i), reverse=reverse, axis=0)
    return jnp.einsum('lgp,ghp->lgh', hr, c_re) - jnp.einsum('lgp,ghp->lgh', hi, c_im)


def s5_mixer(u, a_re, a_im, log_dt, b_re, b_im, c_re, c_im, d, w_glu, b_glu):
    B, Lt = u.shape[0], u.shape[1]
    uf = u.astype(F32).reshape(B, Lt, SSM_GROUPS, SSM_GROUP)
    abr, abi, bbr, bbi = s5_discretize(a_re, a_im, log_dt, b_re, b_im)
    cr = c_re.astype(F32)
    cim = c_im.astype(F32)

    def one_seq(ug):
        return (s5_scan_dir(ug, abr[0], abi[0], bbr[0], bbi[0], cr[0], cim[0], False)
                + s5_scan_dir(ug, abr[1], abi[1], bbr[1], bbi[1], cr[1], cim[1], True))

    y = lax.map(one_seq, uf).reshape(B, Lt, SSM_WIDTH)
    y = y + d.astype(F32) * uf.reshape(B, Lt, SSM_WIDTH)
    y = jax.nn.gelu(y).astype(u.dtype)
    return y * jax.nn.sigmoid(y @ w_glu + b_glu)


def trunk(x, params):
    (meta_tokens, norm1_g, w_in, attn_sink, ssm_a_re, ssm_a_im, ssm_log_dt, ssm_b_re, ssm_b_im,
     ssm_c_re, ssm_c_im, ssm_d, w_glu, b_glu, norm_attn_out, norm_ssm_out, w_out, norm2_g,
     w_mlp_in, w_mlp_out, final_norm_g) = params
    B = x.shape[0]
    meta = jnp.broadcast_to(meta_tokens[None].astype(x.dtype), (B, N_META, D_MODEL))
    x = jnp.concatenate([meta, x], axis=1)
    Lt = x.shape[1]
    pos = jnp.arange(Lt, dtype=F32)
    for l in range(DEPTH):
        h = rmsnorm(x, norm1_g[l])
        z = h @ w_in[l]
        q = z[..., :ATTN_WIDTH].reshape(B, Lt, N_HEADS, HEAD_DIM)
        k = z[..., ATTN_WIDTH:ATTN_WIDTH + KV_WIDTH].reshape(B, Lt, N_KV_HEADS, HEAD_DIM)
        v = z[..., ATTN_WIDTH + KV_WIDTH:ATTN_WIDTH + 2 * KV_WIDTH].reshape(B, Lt, N_KV_HEADS, HEAD_DIM)
        u = z[..., ATTN_WIDTH + 2 * KV_WIDTH:]
        q = partial_rope(q, pos)
        k = partial_rope(k, pos)
        a_out = windowed_attention(q, k, v, attn_sink[l])
        s_out = s5_mixer(u, ssm_a_re[l], ssm_a_im[l], ssm_log_dt[l], ssm_b_re[l], ssm_b_im[l],
                         ssm_c_re[l], ssm_c_im[l], ssm_d[l], w_glu[l], b_glu[l])
        mix = jnp.concatenate([rmsnorm(a_out, norm_attn_out[l]), rmsnorm(s_out, norm_ssm_out[l])], axis=-1)
        x = x + mix @ w_out[l]
        h2 = rmsnorm(x, norm2_g[l])
        x = x + jnp.square(jax.nn.relu(h2 @ w_mlp_in[l])) @ w_mlp_out[l]
    x = rmsnorm(x, final_norm_g)
    return x[:, N_META:]


def setup_inputs(seed: int = 0) -> dict:
    key = jax.random.key(seed)
    ks = jax.random.split(key, 26)

    def nrm(k, shape, scale):
        return jax.random.normal(k, shape, F32) * scale

    G, P, H = SSM_GROUPS, SSM_STATE, SSM_GROUP
    n = jnp.arange(P, dtype=F32)
    return {
        "x_prompt": nrm(ks[0], (BATCH, SEQ, D_MODEL), 1.0),
        "x_sample": nrm(ks[1], (DEC_BATCH, DEC_SEQ, D_MODEL), 1.0),
        "meta_tokens": nrm(ks[2], (N_META, D_MODEL), 1.0),
        "norm1_g": 1.0 + nrm(ks[3], (DEPTH, D_MODEL), 0.02),
        "w_in": nrm(ks[4], (DEPTH, D_MODEL, IN_WIDTH), D_MODEL ** -0.5),
        "attn_sink": nrm(ks[5], (DEPTH, N_HEADS), 0.5),
        "ssm_a_re": -0.5 + nrm(ks[6], (DEPTH, 2, G, P), 0.01),
        "ssm_a_im": jnp.pi * n + nrm(ks[7], (DEPTH, 2, G, P), 0.01),
        "ssm_log_dt": jax.random.uniform(ks[8], (DEPTH, 2, G), F32, minval=math.log(1e-3), maxval=math.log(1e-1)),
        "ssm_b_re": nrm(ks[9], (DEPTH, 2, G, P, H), (2 * H) ** -0.5),
        "ssm_b_im": nrm(ks[10], (DEPTH, 2, G, P, H), (2 * H) ** -0.5),
        "ssm_c_re": nrm(ks[11], (DEPTH, 2, G, H, P), P ** -0.5),
        "ssm_c_im": nrm(ks[12], (DEPTH, 2, G, H, P), P ** -0.5),
        "ssm_d": nrm(ks[13], (DEPTH, SSM_WIDTH), 0.5),
        "w_glu": nrm(ks[14], (DEPTH, SSM_WIDTH, SSM_WIDTH), SSM_WIDTH ** -0.5),
        "b_glu": nrm(ks[15], (DEPTH, SSM_WIDTH), 0.01),
        "norm_attn_out": 1.0 + nrm(ks[16], (DEPTH, ATTN_WIDTH), 0.02),
        "norm_ssm_out": 1.0 + nrm(ks[17], (DEPTH, SSM_WIDTH), 0.02),
        "w_out": nrm(ks[18], (DEPTH, ATTN_WIDTH + SSM_WIDTH, D_MODEL), (ATTN_WIDTH + SSM_WIDTH) ** -0.5),
        "norm2_g": 1.0 + nrm(ks[19], (DEPTH, D_MODEL), 0.02),
        "w_mlp_in": nrm(ks[20], (DEPTH, D_MODEL, D_FF), D_MODEL ** -0.5),
        "w_mlp_out": nrm(ks[21], (DEPTH, D_FF, D_MODEL), D_FF ** -0.5),
        "final_norm_g": 1.0 + nrm(ks[22], (D_MODEL,), 0.02),
    }


def reference(x_prompt, x_sample, meta_tokens, norm1_g, w_in, attn_sink, ssm_a_re, ssm_a_im, ssm_log_dt,
              ssm_b_re, ssm_b_im, ssm_c_re, ssm_c_im, ssm_d, w_glu, b_glu, norm_attn_out, norm_ssm_out,
              w_out, norm2_g, w_mlp_in, w_mlp_out, final_norm_g):
    params = (meta_tokens, norm1_g, w_in, attn_sink, ssm_a_re, ssm_a_im, ssm_log_dt, ssm_b_re, ssm_b_im,
              ssm_c_re, ssm_c_im, ssm_d, w_glu, b_glu, norm_attn_out, norm_ssm_out, w_out, norm2_g,
              w_mlp_in, w_mlp_out, final_norm_g)
    y_prompt = trunk(x_prompt, params)
    y_sample = trunk(x_sample, params)
    return (y_prompt, y_sample)
```

```python
import functools
import math

import jax
import jax.numpy as jnp
from jax import lax
from jax.experimental import pallas as pl
from jax.experimental.pallas import tpu as pltpu

F32 = jnp.float32
BF16 = jnp.bfloat16

LANES = 128
N_META = 16
HEAD_DIM = 64
KV_REP = 4
ROT_DIM = HEAD_DIM // 4
ROPE_THETA = 500000.0
WINDOW = 128
ATTN_TILE = 128
SSM_GROUP = 16
CHUNK = 16
NORM_EPS = 1e-6
NEG_INF = -1e30
VMEM_LIMIT = 56 * 1024 * 1024


def _round_up(n, m):
    return (n + m - 1) // m * m


def _token_tile(lp):
    for t in (768, 384, 256, 128):
        if lp % t == 0:
            return t
    raise ValueError(f"padded length {lp} is not a multiple of 128")


def _rms(x, g):
    return x * lax.rsqrt(jnp.mean(x * x, axis=-1, keepdims=True) + NORM_EPS) * g


def _params(*sem):
    return pltpu.CompilerParams(dimension_semantics=sem, vmem_limit_bytes=VMEM_LIMIT)


def _inproj_kernel(x_ref, g_ref, w_ref, cos_ref, s1_ref, s2_ref, q_ref, k_ref, v_ref, u_ref, *,
                   tq, lt, tiles_per_seq, attn_w, kv_w):
    i = pl.program_id(0)
    h = _rms(x_ref[...], g_ref[...]).astype(BF16)
    z = jnp.dot(h, w_ref[...], preferred_element_type=F32)
    pos = (i % tiles_per_seq) * tq + lax.broadcasted_iota(jnp.int32, (tq, 1), 0)
    z = jnp.where(pos < lt, z, 0.0)
    cos, s1, s2 = cos_ref[...], s1_ref[...], s2_ref[...]
    half = ROT_DIM // 2

    def rope(blk):
        return (blk * cos + pltpu.roll(blk, LANES - half, 1) * s1 + pltpu.roll(blk, half, 1) * s2)

    scale = HEAD_DIM ** -0.5
    for b in range(attn_w // LANES):
        q_ref[:, b * LANES:(b + 1) * LANES] = (rope(z[:, b * LANES:(b + 1) * LANES]) * scale).astype(BF16)
    for b in range(kv_w // LANES):
        lo = attn_w + b * LANES
        k_ref[:, b * LANES:(b + 1) * LANES] = rope(z[:, lo:lo + LANES]).astype(BF16)
    v_ref[...] = z[:, attn_w + kv_w:attn_w + 2 * kv_w].astype(BF16)
    u_ref[...] = z[:, attn_w + 2 * kv_w:].astype(BF16)


def _inproj(x, g, w, tabs, lt, lp, attn_w, kv_w):
    n, d = x.shape
    tq = _token_tile(lp)
    tps = lp // tq
    ssm_w = w.shape[1] - attn_w - 2 * kv_w
    row = lambda i: (i, 0)
    tab = lambda i: (i % tps, 0)
    const = lambda i: (0, 0)
    return pl.pallas_call(
        functools.partial(_inproj_kernel, tq=tq, lt=lt, tiles_per_seq=tps, attn_w=attn_w, kv_w=kv_w),
        grid=(n // tq,),
        in_specs=[pl.BlockSpec((tq, d), row), pl.BlockSpec((1, d), const), pl.BlockSpec(w.shape, const),
                  pl.BlockSpec((tq, LANES), tab), pl.BlockSpec((tq, LANES), tab), pl.BlockSpec((tq, LANES), tab)],
        out_specs=[pl.BlockSpec((tq, attn_w), row), pl.BlockSpec((tq, kv_w), row),
                   pl.BlockSpec((tq, kv_w), row), pl.BlockSpec((tq, ssm_w), row)],
        out_shape=[jax.ShapeDtypeStruct((n, attn_w), BF16), jax.ShapeDtypeStruct((n, kv_w), BF16),
                   jax.ShapeDtypeStruct((n, kv_w), BF16), jax.ShapeDtypeStruct((n, ssm_w), BF16)],
        compiler_params=_params("arbitrary"),
        name="inproj",
    )(x, g, w, *tabs)


def _attn_kernel(sink_ref, q_ref, kp_ref, kc_ref, kn_ref, km_ref, vp_ref, vc_ref, vn_ref, vm_ref, o_ref, *,
                 lt, n_heads):
    a = pl.program_id(1)
    t = ATTN_TILE
    q = q_ref[0]
    kall = jnp.concatenate([kp_ref[0], kc_ref[0], kn_ref[0]], axis=0)
    vall = jnp.concatenate([vp_ref[0], vc_ref[0], vn_ref[0]], axis=0)
    km, vm = km_ref[0], vm_ref[0]
    qpos = a * t + lax.broadcasted_iota(jnp.int32, (t, 1), 0)
    kpos = (a - 1) * t + lax.broadcasted_iota(jnp.int32, (1, 3 * t), 1)
    band = (jnp.abs(qpos - kpos) <= WINDOW) & (kpos >= 0) & (kpos < lt)
    bias = jnp.where(band, 0.0, NEG_INF)
    mpos = lax.broadcasted_iota(jnp.int32, (1, N_META), 1)
    mbias = jnp.where(qpos - mpos > WINDOW, 0.0, NEG_INF)
    nt = (((1,), (1,)), ((), ()))
    for h in range(n_heads):
        g = h // KV_REP
        qh = q[:, h * HEAD_DIM:(h + 1) * HEAD_DIM]
        ks = slice(g * HEAD_DIM, (g + 1) * HEAD_DIM)
        s = lax.dot_general(qh, kall[:, ks], nt, preferred_element_type=F32) + bias
        sm = lax.dot_general(qh, km[:, ks], nt, preferred_element_type=F32) + mbias
        sink = sink_ref[h]
        m = jnp.maximum(jnp.maximum(jnp.max(s, axis=-1, keepdims=True), jnp.max(sm, axis=-1, keepdims=True)), sink)
        e = jnp.exp(s - m)
        em = jnp.exp(sm - m)
        den = jnp.sum(e, axis=-1, keepdims=True) + jnp.sum(em, axis=-1, keepdims=True) + jnp.exp(sink - m)
        o = (jnp.dot(e.astype(BF16), vall[:, ks], preferred_element_type=F32)
             + jnp.dot(em.astype(BF16), vm[:, ks], preferred_element_type=F32))
        o_ref[0, :, h * HEAD_DIM:(h + 1) * HEAD_DIM] = (o / den).astype(BF16)


def _attention(q, k, v, sink, lt):
    b, lp, attn_w = q.shape
    kv_w = k.shape[2]
    t = ATTN_TILE
    nt = lp // t
    cur = lambda bi, a: (bi, a, 0)
    prev = lambda bi, a: (bi, jnp.maximum(a - 1, 0), 0)
    nxt = lambda bi, a: (bi, jnp.minimum(a + 1, nt - 1), 0)
    meta = lambda bi, a: (bi, 0, 0)
    kv = lambda m: pl.BlockSpec((1, t, kv_w), m)
    mspec = pl.BlockSpec((1, N_META, kv_w), meta)
    return pl.pallas_call(
        functools.partial(_attn_kernel, lt=lt, n_heads=attn_w // HEAD_DIM),
        grid=(b, nt),
        in_specs=[pl.BlockSpec(memory_space=pltpu.SMEM), pl.BlockSpec((1, t, attn_w), cur),
                  kv(prev), kv(cur), kv(nxt), mspec, kv(prev), kv(cur), kv(nxt), mspec],
        out_specs=pl.BlockSpec((1, t, attn_w), cur),
        out_shape=jax.ShapeDtypeStruct((b, lp, attn_w), BF16),
        compiler_params=_params("arbitrary", "arbitrary"),
        name="attention",
    )(sink, q, k, k, k, k, v, v, v, v)


def _ssm_prep(a_re, a_im, log_dt, b_re, b_im, c_re, c_im, d, nsteps):
    hp = lax.Precision.HIGHEST
    t = CHUNK
    a_re, a_im, b_re, b_im = (v.astype(F32) for v in (a_re, a_im, b_re, b_im))
    c_re, c_im = c_re.astype(F32), c_im.astype(F32)
    n_g, n_p, n_h = b_re.shape[1], b_re.shape[2], b_re.shape[3]
    dt = jnp.exp(log_dt.astype(F32))[..., None]

    def cpow(j):
        j = j.astype(F32)[:, None, None, None]
        mag = jnp.exp(a_re * dt * j)
        ang = a_im * dt * j
        return mag * jnp.cos(ang), mag * jnp.sin(ang)

    mag = jnp.exp(a_re * dt)
    ang = a_im * dt
    abr, abi = mag * jnp.cos(ang), mag * jnp.sin(ang)
    den = a_re * a_re + a_im * a_im
    nr = abr - 1.0
    cr = (nr * a_re + abi * a_im) / den
    ci = (abi * a_re - nr * a_im) / den
    bbr = cr[..., None] * b_re - ci[..., None] * b_im
    bbi = cr[..., None] * b_im + ci[..., None] * b_re

    pr, pi = cpow(jnp.arange(t + 1))
    car = c_re[None] * pr[:, :, :, None, :] - c_im[None] * pi[:, :, :, None, :]
    cai = c_re[None] * pi[:, :, :, None, :] + c_im[None] * pr[:, :, :, None, :]
    kj = (jnp.einsum('jdghp,dgpk->jdghk', car[:t], bbr, precision=hp)
          - jnp.einsum('jdghp,dgpk->jdghk', cai[:t], bbi, precision=hp))
    kf, kb = kj[:, 0], kj[:, 1]
    lag = jnp.arange(t)[:, None] - jnp.arange(t)[None, :]
    m = jnp.where((lag >= 0)[:, :, None, None, None], kf[jnp.abs(lag)], 0.0) \
        + jnp.where((lag <= 0)[:, :, None, None, None], kb[jnp.abs(lag)], 0.0)
    dmat = jnp.eye(n_h, dtype=F32)[None] * d.astype(F32).reshape(n_g, n_h, 1)
    m = m + jnp.eye(t, dtype=F32)[:, :, None, None, None] * dmat[None, None]
    mt = jnp.transpose(m, (2, 1, 4, 0, 3)).reshape(n_g, t * n_h, t * n_h)

    def b_pow(pw_r, pw_i, d_idx):
        re = pw_r[:, :, :, None] * bbr[d_idx][None] - pw_i[:, :, :, None] * bbi[d_idx][None]
        im = pw_r[:, :, :, None] * bbi[d_idx][None] + pw_i[:, :, :, None] * bbr[d_idx][None]
        tr = lambda v: jnp.transpose(v, (1, 0, 3, 2)).reshape(n_g, t * n_h, n_p)
        return tr(re), tr(im)

    wf_r, wf_i = b_pow(pr[:t, 0][::-1], pi[:t, 0][::-1], 0)
    wb_r, wb_i = b_pow(pr[:t, 1], pi[:t, 1], 1)
    ws = jnp.concatenate([wf_r, wf_i, wb_r, wb_i], axis=-1)

    def c_rows(re, im):
        tr = lambda v: jnp.transpose(v, (1, 3, 0, 2)).reshape(n_g, n_p, t * n_h)
        return tr(re), tr(-im)

    of_r, of_i = c_rows(car[1:, 0], cai[1:, 0])
    ob_r, ob_i = c_rows(car[1:, 1][::-1], cai[1:, 1][::-1])
    wo = jnp.concatenate([of_r, of_i, ob_r, ob_i], axis=1)

    sr, si = cpow(t * (2 ** jnp.arange(nsteps)))

    def pat(d_idx):
        a1 = jnp.concatenate([sr[:, d_idx], sr[:, d_idx]], axis=-1)
        a2 = jnp.concatenate([-si[:, d_idx], si[:, d_idx]], axis=-1)
        return jnp.transpose(jnp.stack([a1, a2], axis=1), (2, 0, 1, 3)).reshape(n_g, 2 * nsteps, 2 * n_p)

    return mt.astype(BF16), ws.astype(BF16), wo.astype(BF16), pat(0), pat(1)


def _ssm_kernel(x_ref, mt_ref, ws_ref, wo_ref, pf_ref, pb_ref, y_ref, fbuf, bbuf, *, nc, nsteps, pad):
    x = x_ref[0, 0]
    s = jnp.dot(x, ws_ref[0], preferred_element_type=F32)
    hw = s.shape[1] // 2
    zeros = jnp.zeros((pad, hw), F32)
    fbuf[0:pad, :] = zeros
    bbuf[nc:nc + pad, :] = zeros
    fbuf[pad:pad + nc, :] = s[:, :hw]
    bbuf[0:nc, :] = s[:, hw:]
    d = 1
    for k in range(nsteps):
        cur = fbuf[pad:pad + nc, :]
        sh = fbuf[pad - d:pad - d + nc, :]
        fbuf[pad:pad + nc, :] = (cur + pf_ref[0, 2 * k:2 * k + 1, :] * sh
                                 + pf_ref[0, 2 * k + 1:2 * k + 2, :] * pltpu.roll(sh, hw // 2, 1))
        cur = bbuf[0:nc, :]
        sh = bbuf[d:d + nc, :]
        bbuf[0:nc, :] = (cur + pb_ref[0, 2 * k:2 * k + 1, :] * sh
                         + pb_ref[0, 2 * k + 1:2 * k + 2, :] * pltpu.roll(sh, hw // 2, 1))
        d *= 2
    hin = jnp.concatenate([fbuf[pad - 1:pad - 1 + nc, :], bbuf[1:1 + nc, :]], axis=1).astype(BF16)
    y = (jnp.dot(x, mt_ref[0], preferred_element_type=F32)
         + jnp.dot(hin, wo_ref[0], preferred_element_type=F32))
    y_ref[0, 0] = y.astype(BF16)


def _ssm(xg, mt, ws, wo, pf, pb, nsteps):
    b, n_g, nc, kw = xg.shape
    pad = max(8, 2 ** (nsteps - 1))
    grp = lambda bi, g: (g, 0, 0)
    io = lambda bi, g: (bi, g, 0, 0)
    hw = ws.shape[2] // 2
    return pl.pallas_call(
        functools.partial(_ssm_kernel, nc=nc, nsteps=nsteps, pad=pad),
        grid=(b, n_g),
        in_specs=[pl.BlockSpec((1, 1, nc, kw), io), pl.BlockSpec((1,) + mt.shape[1:], grp),
                  pl.BlockSpec((1,) + ws.shape[1:], grp), pl.BlockSpec((1,) + wo.shape[1:], grp),
                  pl.BlockSpec((1,) + pf.shape[1:], grp), pl.BlockSpec((1,) + pb.shape[1:], grp)],
        out_specs=pl.BlockSpec((1, 1, nc, kw), io),
        out_shape=jax.ShapeDtypeStruct(xg.shape, BF16),
        scratch_shapes=[pltpu.VMEM((nc + pad, hw), F32), pltpu.VMEM((nc + pad, hw), F32)],
        compiler_params=_params("arbitrary", "arbitrary"),
        name="ssm",
    )(xg, mt, ws, wo, pf, pb)


def _gelu_tanh(x):
    return 0.5 * x * (1.0 + jnp.tanh(math.sqrt(2.0 / math.pi) * (x + 0.044715 * (x * x * x))))


def _mix_kernel(x_ref, a_ref, y_ref, wglu_ref, bglu_ref, ga_ref, gs_ref, woa_ref, wos_ref, o_ref):
    yg = _gelu_tanh(y_ref[...].astype(F32))
    gate = jax.nn.sigmoid(jnp.dot(yg.astype(BF16), wglu_ref[...], preferred_element_type=F32) + bglu_ref[...])
    sn = _rms(yg * gate, gs_ref[...]).astype(BF16)
    an = _rms(a_ref[...].astype(F32), ga_ref[...]).astype(BF16)
    o_ref[...] = (x_ref[...] + jnp.dot(an, woa_ref[...], preferred_element_type=F32)
                  + jnp.dot(sn, wos_ref[...], preferred_element_type=F32))


def _mix(x, a, y, wglu, bglu, ga, gs, woa, wos, lp):
    n, d = x.shape
    tq = _token_tile(lp)
    row = lambda i: (i, 0)
    const = lambda i: (0, 0)
    full = lambda arr: pl.BlockSpec(arr.shape, const)
    return pl.pallas_call(
        _mix_kernel,
        grid=(n // tq,),
        in_specs=[pl.BlockSpec((tq, d), row), pl.BlockSpec((tq, a.shape[1]), row), pl.BlockSpec((tq, y.shape[1]), row),
                  full(wglu), full(bglu), full(ga), full(gs), full(woa), full(wos)],
        out_specs=pl.BlockSpec((tq, d), row),
        out_shape=jax.ShapeDtypeStruct((n, d), F32),
        input_output_aliases={0: 0},
        compiler_params=_params("arbitrary"),
        name="mix",
    )(x, a, y, wglu, bglu, ga, gs, woa, wos)


def _mlp_kernel(x_ref, g_ref, w1_ref, w2_ref, o_ref, h_ref):
    @pl.when(pl.program_id(1) == 0)
    def _():
        x = x_ref[...]
        h_ref[...] = _rms(x, g_ref[...]).astype(BF16)
        o_ref[...] = x

    t = jnp.dot(h_ref[...], w1_ref[...], preferred_element_type=F32)
    t = jnp.square(jnp.maximum(t, 0.0)).astype(BF16)
    o_ref[...] += jnp.dot(t, w2_ref[...], preferred_element_type=F32)


def _mlp(x, g, w1, w2, lp, tf=512):
    n, d = x.shape
    dff = w1.shape[1]
    tq = _token_tile(lp)
    row = lambda i, j: (i, 0)
    return pl.pallas_call(
        _mlp_kernel,
        grid=(n // tq, dff // tf),
        in_specs=[pl.BlockSpec((tq, d), row), pl.BlockSpec((1, d), lambda i, j: (0, 0)),
                  pl.BlockSpec((d, tf), lambda i, j: (0, j)), pl.BlockSpec((tf, d), lambda i, j: (j, 0))],
        out_specs=pl.BlockSpec((tq, d), row),
        out_shape=jax.ShapeDtypeStruct((n, d), F32),
        scratch_shapes=[pltpu.VMEM((tq, d), BF16)],
        input_output_aliases={0: 0},
        compiler_params=_params("arbitrary", "arbitrary"),
        name="mlp",
    )(x, g, w1, w2)


def _final_kernel(x_ref, g_ref, o_ref):
    o_ref[...] = _rms(x_ref[...], g_ref[...])


def _final_norm(x, g, lp):
    n, d = x.shape
    tq = _token_tile(lp)
    row = lambda i: (i, 0)
    return pl.pallas_call(
        _final_kernel,
        grid=(n // tq,),
        in_specs=[pl.BlockSpec((tq, d), row), pl.BlockSpec((1, d), lambda i: (0, 0))],
        out_specs=pl.BlockSpec((tq, d), row),
        out_shape=jax.ShapeDtypeStruct((n, d), F32),
        compiler_params=_params("arbitrary"),
        name="final_norm",
    )(x, g)


def _rope_tables(lp):
    half = ROT_DIM // 2
    pos = jnp.arange(lp, dtype=F32)
    inv = ROPE_THETA ** (-jnp.arange(half, dtype=F32) * 2.0 / ROT_DIM)
    ang = pos[:, None] * inv[None, :]
    cos, sin = jnp.cos(ang), jnp.sin(ang)
    rest = HEAD_DIM - ROT_DIM
    c = jnp.concatenate([cos, cos, jnp.ones((lp, rest), F32)], axis=1)
    s1 = jnp.concatenate([-sin, jnp.zeros((lp, HEAD_DIM - half), F32)], axis=1)
    s2 = jnp.concatenate([jnp.zeros((lp, half), F32), sin, jnp.zeros((lp, rest), F32)], axis=1)
    rep = LANES // HEAD_DIM
    return tuple(jnp.tile(v, (1, rep)) for v in (c, s1, s2))


def _trunk(x_in, layers, meta_tokens, final_g):
    b, s, d = x_in.shape
    lt = N_META + s
    lp = _round_up(lt, ATTN_TILE)
    nc = lp // CHUNK
    meta = jnp.broadcast_to(meta_tokens[None].astype(F32), (b, N_META, d))
    x = jnp.concatenate([meta, x_in, jnp.zeros((b, lp - lt, d), F32)], axis=1).reshape(b * lp, d)
    tabs = _rope_tables(lp)
    for p in layers:
        attn_w = p["attn_w"]
        kv_w = p["kv_w"]
        ssm_w = p["ssm_w"]
        n_g = ssm_w // SSM_GROUP
        q, k, v, u = _inproj(x, p["g1"], p["w_in"], tabs, lt, lp, attn_w, kv_w)
        a = _attention(q.reshape(b, lp, attn_w), k.reshape(b, lp, kv_w), v.reshape(b, lp, kv_w), p["sink"], lt)
        xg = u.reshape(b, nc, CHUNK, n_g, SSM_GROUP).transpose(0, 3, 1, 2, 4).reshape(b, n_g, nc, CHUNK * SSM_GROUP)
        nsteps = p["nsteps"][nc]
        yg = _ssm(xg, *p["ssm"][nc], nsteps)
        y = yg.reshape(b, n_g, nc, CHUNK, SSM_GROUP).transpose(0, 2, 3, 1, 4).reshape(b * lp, ssm_w)
        x = _mix(x, a.reshape(b * lp, attn_w), y, p["w_glu"], p["b_glu"], p["g_attn"], p["g_ssm"],
                 p["w_out_a"], p["w_out_s"], lp)
        x = _mlp(x, p["g2"], p["w1"], p["w2"], lp)
    x = _final_norm(x, final_g, lp)
    return x.reshape(b, lp, d)[:, N_META:lt]


def kernel(x_prompt, x_sample, meta_tokens, norm1_g, w_in, attn_sink, ssm_a_re, ssm_a_im, ssm_log_dt, ssm_b_re, ssm_b_im, ssm_c_re, ssm_c_im, ssm_d, w_glu, b_glu, norm_attn_out, norm_ssm_out, w_out, norm2_g, w_mlp_in, w_mlp_out, final_norm_g):
    depth, d, in_w = w_in.shape
    attn_w = norm_attn_out.shape[1]
    ssm_w = norm_ssm_out.shape[1]
    kv_w = (in_w - attn_w - ssm_w) // 2
    ncs = sorted({_round_up(N_META + xs.shape[1], ATTN_TILE) // CHUNK for xs in (x_prompt, x_sample)})
    layers = []
    for l in range(depth):
        nsteps = {nc: max(1, math.ceil(math.log2(nc))) for nc in ncs}
        ssm = {nc: _ssm_prep(ssm_a_re[l], ssm_a_im[l], ssm_log_dt[l], ssm_b_re[l], ssm_b_im[l],
                             ssm_c_re[l], ssm_c_im[l], ssm_d[l], nsteps[nc]) for nc in ncs}
        layers.append(dict(
            attn_w=attn_w, kv_w=kv_w, ssm_w=ssm_w, nsteps=nsteps, ssm=ssm,
            g1=norm1_g[l].astype(F32)[None], w_in=w_in[l].astype(BF16), sink=attn_sink[l].astype(F32),
            w_glu=w_glu[l].astype(BF16), b_glu=b_glu[l].astype(F32)[None],
            g_attn=norm_attn_out[l].astype(F32)[None], g_ssm=norm_ssm_out[l].astype(F32)[None],
            w_out_a=w_out[l, :attn_w].astype(BF16), w_out_s=w_out[l, attn_w:].astype(BF16),
            g2=norm2_g[l].astype(F32)[None], w1=w_mlp_in[l].astype(BF16), w2=w_mlp_out[l].astype(BF16)))
    fg = final_norm_g.astype(F32)[None]
    return (_trunk(x_prompt, layers, meta_tokens, fg), _trunk(x_sample, layers, meta_tokens, fg))
```

```python
import functools
import math

import jax
import jax.numpy as jnp
from jax import lax
from jax.experimental import pallas as pl
from jax.experimental.pallas import tpu as pltpu

F32 = jnp.float32
BF16 = jnp.bfloat16

LANES = 128
N_META = 16
HEAD_DIM = 64
KV_REP = 4
ROT_DIM = HEAD_DIM // 4
ROPE_THETA = 500000.0
WINDOW = 128
ATTN_TILE = 128
SSM_GROUP = 16
CHUNK = 16
BLOCKS_PER_VREG = LANES // SSM_GROUP
NORM_EPS = 1e-6
NEG_INF = -1e30
VMEM_LIMIT = 56 * 1024 * 1024
DENSE_TILES = (768, 384, 256, 128)
MIX_TILES = (384, 256, 128)
N_CHUNK_COLS = 512


def _round_up(n, m):
    return (n + m - 1) // m * m


def _tile(n, candidates):
    for t in candidates:
        if n % t == 0:
            return t
    raise ValueError(f"{n} rows cannot be tiled by any of {candidates}")


def _rms(x, g):
    return x * lax.rsqrt(jnp.mean(x * x, axis=-1, keepdims=True) + NORM_EPS) * g


def _params(*sem):
    return pltpu.CompilerParams(dimension_semantics=sem, vmem_limit_bytes=VMEM_LIMIT)


def _resident(arr):
    nd = arr.ndim
    return pl.BlockSpec(arr.shape, lambda *_: (0,) * nd, pipeline_mode=pl.Buffered(1))


def _block_transpose(parts):
    blk = lax.broadcasted_iota(jnp.int32, parts[0].shape, 1) // SSM_GROUP
    b = BLOCKS_PER_VREG // 2
    while b:
        upper = (blk & b) != 0
        nxt = list(parts)
        for i in range(len(parts)):
            if i & b == 0:
                lo, hi = parts[i], parts[i + b]
                nxt[i] = jnp.where(upper, pltpu.roll(hi, SSM_GROUP * b, 1), lo)
                nxt[i + b] = jnp.where(upper, hi, pltpu.roll(lo, LANES - SSM_GROUP * b, 1))
        parts = nxt
        b //= 2
    return parts


def _inproj_kernel(x_ref, g_ref, w_ref, wvt_ref, tab_ref, q_ref, k_ref, vt_ref, xg_ref, ubuf, *, attn_w, kv_w):
    tq, d = x_ref.shape
    cos, s1, s2 = tab_ref[:, 0:LANES], tab_ref[:, LANES:2 * LANES], tab_ref[:, 2 * LANES:3 * LANES]
    valid = tab_ref[:, 3 * LANES:4 * LANES] > 0.0
    h = jnp.where(jnp.concatenate([valid] * (d // LANES), axis=1), _rms(x_ref[...], g_ref[...]), 0.0).astype(BF16)
    half = ROT_DIM // 2
    scale = HEAD_DIM ** -0.5

    def rope(blk):
        return blk * cos + pltpu.roll(blk, LANES - half, 1) * s1 + pltpu.roll(blk, half, 1) * s2

    u0 = attn_w + 2 * kv_w
    spans = [(c, min(N_CHUNK_COLS, attn_w + kv_w - c)) for c in range(0, attn_w + kv_w, N_CHUNK_COLS)]
    spans += [(c, min(N_CHUNK_COLS, w_ref.shape[1] - c)) for c in range(u0, w_ref.shape[1], N_CHUNK_COLS)]
    for c0, width in spans:
        z = jnp.dot(h, w_ref[:, c0:c0 + width], preferred_element_type=F32)
        for j in range(width // LANES):
            col = c0 + j * LANES
            blk = z[:, j * LANES:(j + 1) * LANES]
            if col < attn_w:
                q_ref[:, col:col + LANES] = (rope(blk) * scale).astype(BF16)
            elif col < attn_w + kv_w:
                k_ref[:, col - attn_w:col - attn_w + LANES] = rope(blk).astype(BF16)
            else:
                ubuf[(col - u0) // LANES] = blk
    vt_ref[...] = lax.dot_general(wvt_ref[...], h, (((1,), (1,)), ((), ())),
                                  preferred_element_type=F32).astype(BF16)

    nch = tq // CHUNK
    per = 2 * BLOCKS_PER_VREG
    for gb in range(ubuf.shape[0] // 2):
        lo = [ubuf[2 * gb, pl.ds(s, nch, stride=CHUNK), :] for s in range(CHUNK)]
        hi = [ubuf[2 * gb + 1, pl.ds(s, nch, stride=CHUNK), :] for s in range(CHUNK)]
        a, b = _block_transpose(lo[:BLOCKS_PER_VREG]), _block_transpose(lo[BLOCKS_PER_VREG:])
        c, d = _block_transpose(hi[:BLOCKS_PER_VREG]), _block_transpose(hi[BLOCKS_PER_VREG:])
        for i in range(BLOCKS_PER_VREG):
            xg_ref[gb * per + i, :, 0:LANES] = a[i]
            xg_ref[gb * per + i, :, LANES:2 * LANES] = b[i]
            xg_ref[gb * per + BLOCKS_PER_VREG + i, :, 0:LANES] = c[i]
            xg_ref[gb * per + BLOCKS_PER_VREG + i, :, LANES:2 * LANES] = d[i]


def _inproj(x, g, w, wvt, tab, attn_w, kv_w):
    n, d = x.shape
    tq = _tile(n, DENSE_TILES)
    ssm_w = w.shape[1] - attn_w - 2 * kv_w
    n_g = ssm_w // SSM_GROUP
    row = lambda i: (i, 0)
    return pl.pallas_call(
        functools.partial(_inproj_kernel, attn_w=attn_w, kv_w=kv_w),
        grid=(n // tq,),
        in_specs=[pl.BlockSpec((tq, d), row), _resident(g), _resident(w), _resident(wvt),
                  pl.BlockSpec((tq, 4 * LANES), row)],
        out_specs=[pl.BlockSpec((tq, attn_w), row), pl.BlockSpec((tq, kv_w), row),
                   pl.BlockSpec((kv_w, tq), lambda i: (0, i)),
                   pl.BlockSpec((n_g, tq // CHUNK, CHUNK * SSM_GROUP), lambda i: (0, i, 0))],
        out_shape=[jax.ShapeDtypeStruct((n, attn_w), BF16), jax.ShapeDtypeStruct((n, kv_w), BF16),
                   jax.ShapeDtypeStruct((kv_w, n), BF16),
                   jax.ShapeDtypeStruct((n_g, n // CHUNK, CHUNK * SSM_GROUP), F32)],
        scratch_shapes=[pltpu.VMEM((ssm_w // LANES, tq, LANES), F32)],
        compiler_params=_params("arbitrary"),
        name="inproj",
    )(x, g, w, wvt, tab)


def _attn_kernel(sink_ref, q_ref, kp_ref, kc_ref, kn_ref, k0_ref, vp_ref, vc_ref, vn_ref, v0_ref, o_ref, *,
                 lt, n_heads):
    a = pl.program_id(1)
    t = ATTN_TILE
    q = q_ref[0]
    kall = jnp.concatenate([kp_ref[0], kc_ref[0], kn_ref[0], k0_ref[0]], axis=0)
    vtall = jnp.concatenate([vp_ref[...], vc_ref[...], vn_ref[...], v0_ref[...]], axis=1)
    qpos = a * t + lax.broadcasted_iota(jnp.int32, (3 * t, t), 1)
    kpos = (a - 1) * t + lax.broadcasted_iota(jnp.int32, (3 * t, t), 0)
    band = (jnp.abs(qpos - kpos) <= WINDOW) & (kpos >= 0) & (kpos < lt)
    mpos = lax.broadcasted_iota(jnp.int32, (t, t), 0)
    far_meta = (mpos < N_META) & (a * t + lax.broadcasted_iota(jnp.int32, (t, t), 1) - mpos > WINDOW)
    bias = jnp.where(jnp.concatenate([band, far_meta], axis=0), 0.0, NEG_INF)
    bias = jnp.concatenate([bias] * KV_REP, axis=1)
    nt = (((1,), (1,)), ((), ()))
    for g in range(n_heads // KV_REP):
        heads = range(g * KV_REP, (g + 1) * KV_REP)
        qg = jnp.concatenate([q[:, h * HEAD_DIM:(h + 1) * HEAD_DIM] for h in heads], axis=0)
        ks = slice(g * HEAD_DIM, (g + 1) * HEAD_DIM)
        s = lax.dot_general(kall[:, ks], qg, nt, preferred_element_type=F32) + bias
        sink = jnp.concatenate([jnp.full((1, t), sink_ref[h], F32) for h in heads], axis=1)
        m = jnp.maximum(jnp.max(s, axis=0, keepdims=True), sink)
        e = jnp.exp(s - m)
        den = jnp.sum(e, axis=0, keepdims=True) + jnp.exp(sink - m)
        o = jnp.dot(vtall[ks, :], e.astype(BF16), preferred_element_type=F32) * (1.0 / den)
        for r in range(0, KV_REP, 2):
            pair = jnp.concatenate([o[:, r * t:(r + 1) * t], o[:, (r + 1) * t:(r + 2) * t]], axis=0)
            lo = (g * KV_REP + r) * HEAD_DIM
            o_ref[0, :, lo:lo + 2 * HEAD_DIM] = pair.T.astype(BF16)


def _attention(q, k, vt, sink, lt):
    b, lp, attn_w = q.shape
    kv_w = k.shape[2]
    t = ATTN_TILE
    nt = lp // t
    cur = lambda bi, a: (bi, a, 0)
    prev = lambda bi, a: (bi, jnp.maximum(a - 1, 0), 0)
    nxt = lambda bi, a: (bi, jnp.minimum(a + 1, nt - 1), 0)
    first = lambda bi, a: (bi, 0, 0)
    kspec = lambda m: pl.BlockSpec((1, t, kv_w), m)
    vspec = lambda m: pl.BlockSpec((kv_w, t), lambda bi, a: (0, bi * nt + m(bi, a)[1]))
    return pl.pallas_call(
        functools.partial(_attn_kernel, lt=lt, n_heads=attn_w // HEAD_DIM),
        grid=(b, nt),
        in_specs=[pl.BlockSpec(memory_space=pltpu.SMEM), pl.BlockSpec((1, t, attn_w), cur),
                  kspec(prev), kspec(cur), kspec(nxt), kspec(first),
                  vspec(prev), vspec(cur), vspec(nxt), vspec(first)],
        out_specs=pl.BlockSpec((1, t, attn_w), cur),
        out_shape=jax.ShapeDtypeStruct((b, lp, attn_w), BF16),
        compiler_params=_params("arbitrary", "arbitrary"),
        name="attention",
    )(sink, q, k, k, k, k, vt, vt, vt, vt)


def _ssm_prep(a_re, a_im, log_dt, b_re, b_im, c_re, c_im, d, nsteps):
    hp = lax.Precision.HIGHEST
    t = CHUNK
    a_re, a_im, b_re, b_im = (v.astype(F32) for v in (a_re, a_im, b_re, b_im))
    c_re, c_im = c_re.astype(F32), c_im.astype(F32)
    n_l, _, n_g, n_p, n_h = b_re.shape
    dt = jnp.exp(log_dt.astype(F32))[..., None]

    def cpow(j):
        j = j.astype(F32)[:, None, None, None, None]
        mag = jnp.exp(a_re * dt * j)
        ang = a_im * dt * j
        return mag * jnp.cos(ang), mag * jnp.sin(ang)

    mag = jnp.exp(a_re * dt)
    ang = a_im * dt
    abr, abi = mag * jnp.cos(ang), mag * jnp.sin(ang)
    den = a_re * a_re + a_im * a_im
    nr = abr - 1.0
    cr = (nr * a_re + abi * a_im) / den
    ci = (abi * a_re - nr * a_im) / den
    bbr = cr[..., None] * b_re - ci[..., None] * b_im
    bbi = cr[..., None] * b_im + ci[..., None] * b_re

    pr, pi = cpow(jnp.arange(t + 1))
    car = c_re[None] * pr[..., None, :] - c_im[None] * pi[..., None, :]
    cai = c_re[None] * pi[..., None, :] + c_im[None] * pr[..., None, :]
    kj = (jnp.einsum('jldghp,ldgpk->jldghk', car[:t], bbr, precision=hp)
          - jnp.einsum('jldghp,ldgpk->jldghk', cai[:t], bbi, precision=hp))
    kf, kb = kj[:, :, 0], kj[:, :, 1]
    lag = jnp.arange(t)[:, None] - jnp.arange(t)[None, :]
    sel = lambda cond: cond[:, :, None, None, None, None]
    m = jnp.where(sel(lag >= 0), kf[jnp.abs(lag)], 0.0) + jnp.where(sel(lag <= 0), kb[jnp.abs(lag)], 0.0)
    dmat = jnp.eye(n_h, dtype=F32) * d.astype(F32).reshape(n_l, n_g, n_h, 1)
    m = m + sel(lag == 0) * dmat[None, None]
    mt = jnp.transpose(m, (2, 3, 1, 5, 0, 4)).reshape(n_l, n_g, t * n_h, t * n_h)

    def b_pow(pw_r, pw_i, d_idx):
        br, bi = bbr[:, d_idx][None], bbi[:, d_idx][None]
        re = pw_r[..., None] * br - pw_i[..., None] * bi
        im = pw_r[..., None] * bi + pw_i[..., None] * br
        tr = lambda v: jnp.transpose(v, (1, 2, 0, 4, 3)).reshape(n_l, n_g, t * n_h, n_p)
        return tr(re), tr(im)

    wf_r, wf_i = b_pow(pr[:t, :, 0][::-1], pi[:t, :, 0][::-1], 0)
    wb_r, wb_i = b_pow(pr[:t, :, 1], pi[:t, :, 1], 1)
    ws = jnp.concatenate([wf_r, wf_i, wb_r, wb_i], axis=-1)

    def c_rows(re, im):
        tr = lambda v: jnp.transpose(v, (1, 2, 4, 0, 3)).reshape(n_l, n_g, n_p, t * n_h)
        return tr(re), tr(-im)

    of_r, of_i = c_rows(car[1:, :, 0], cai[1:, :, 0])
    ob_r, ob_i = c_rows(car[1:, :, 1][::-1], cai[1:, :, 1][::-1])
    wo = jnp.concatenate([of_r, of_i, ob_r, ob_i], axis=2)

    sr, si = cpow(t * (2 ** jnp.arange(nsteps)))

    def pat(d_idx):
        a1 = jnp.concatenate([sr[:, :, d_idx], sr[:, :, d_idx]], axis=-1)
        a2 = jnp.concatenate([-si[:, :, d_idx], si[:, :, d_idx]], axis=-1)
        both = jnp.stack([a1, a2], axis=1)
        return jnp.transpose(both, (2, 3, 0, 1, 4)).reshape(n_l, n_g, 2 * nsteps, 2 * n_p)

    return mt.astype(BF16), ws.astype(BF16), wo.astype(BF16), pat(0), pat(1)


def _ssm_kernel(x_ref, mt_ref, ws_ref, wo_ref, pf_ref, pb_ref, y_ref, fbuf, bbuf, *, nc, nsteps, pad):
    x = x_ref[0].astype(BF16)
    s = jnp.dot(x, ws_ref[0], preferred_element_type=F32)
    hw = s.shape[1] // 2
    zeros = jnp.zeros((pad, hw), F32)
    fbuf[0:pad, :] = zeros
    bbuf[nc:nc + pad, :] = zeros
    fbuf[pad:pad + nc, :] = s[:, :hw]
    bbuf[0:nc, :] = s[:, hw:]
    d = 1
    for k in range(nsteps):
        cur = fbuf[pad:pad + nc, :]
        sh = fbuf[pad - d:pad - d + nc, :]
        fbuf[pad:pad + nc, :] = (cur + pf_ref[0, 2 * k:2 * k + 1, :] * sh
                                 + pf_ref[0, 2 * k + 1:2 * k + 2, :] * pltpu.roll(sh, hw // 2, 1))
        cur = bbuf[0:nc, :]
        sh = bbuf[d:d + nc, :]
        bbuf[0:nc, :] = (cur + pb_ref[0, 2 * k:2 * k + 1, :] * sh
                         + pb_ref[0, 2 * k + 1:2 * k + 2, :] * pltpu.roll(sh, hw // 2, 1))
        d *= 2
    hin = jnp.concatenate([fbuf[pad - 1:pad - 1 + nc, :], bbuf[1:1 + nc, :]], axis=1).astype(BF16)
    y_ref[0] = (jnp.dot(x, mt_ref[0], preferred_element_type=F32)
                + jnp.dot(hin, wo_ref[0], preferred_element_type=F32))


def _ssm(xg, mt, ws, wo, pf, pb, nc):
    n_g, rows, kw = xg.shape
    nsteps = max(1, math.ceil(math.log2(nc)))
    pad = max(8, 2 ** (nsteps - 1))
    grp = lambda bi, g: (g, 0, 0)
    io = lambda bi, g: (g, bi, 0)
    hw = ws.shape[2] // 2
    return pl.pallas_call(
        functools.partial(_ssm_kernel, nc=nc, nsteps=nsteps, pad=pad),
        grid=(rows // nc, n_g),
        in_specs=[pl.BlockSpec((1, nc, kw), io), pl.BlockSpec((1,) + mt.shape[1:], grp),
                  pl.BlockSpec((1,) + ws.shape[1:], grp), pl.BlockSpec((1,) + wo.shape[1:], grp),
                  pl.BlockSpec((1,) + pf.shape[1:], grp), pl.BlockSpec((1,) + pb.shape[1:], grp)],
        out_specs=pl.BlockSpec((1, nc, kw), io),
        out_shape=jax.ShapeDtypeStruct(xg.shape, F32),
        scratch_shapes=[pltpu.VMEM((nc + pad, hw), F32), pltpu.VMEM((nc + pad, hw), F32)],
        compiler_params=_params("arbitrary", "arbitrary"),
        name="ssm",
    )(xg, mt, ws, wo, pf, pb)


def _gelu_tanh(x):
    return 0.5 * x * (1.0 + jnp.tanh(math.sqrt(2.0 / math.pi) * (x + 0.044715 * (x * x * x))))


def _mix_kernel(x_ref, a_ref, yg_ref, wglu_ref, bglu_ref, ga_ref, gs_ref, woa_ref, wos_ref, o_ref, ybuf):
    nch = yg_ref.shape[1]
    per = 2 * BLOCKS_PER_VREG
    for gb in range(ybuf.shape[0] // 2):
        grp = lambda i, half: yg_ref[gb * per + i, :, half * LANES:(half + 1) * LANES]
        a = _block_transpose([grp(i, 0) for i in range(BLOCKS_PER_VREG)])
        b = _block_transpose([grp(i, 1) for i in range(BLOCKS_PER_VREG)])
        c = _block_transpose([grp(BLOCKS_PER_VREG + i, 0) for i in range(BLOCKS_PER_VREG)])
        d = _block_transpose([grp(BLOCKS_PER_VREG + i, 1) for i in range(BLOCKS_PER_VREG)])
        for s in range(BLOCKS_PER_VREG):
            ybuf[2 * gb, pl.ds(s, nch, stride=CHUNK), :] = a[s]
            ybuf[2 * gb, pl.ds(BLOCKS_PER_VREG + s, nch, stride=CHUNK), :] = b[s]
            ybuf[2 * gb + 1, pl.ds(s, nch, stride=CHUNK), :] = c[s]
            ybuf[2 * gb + 1, pl.ds(BLOCKS_PER_VREG + s, nch, stride=CHUNK), :] = d[s]
    yg = jnp.concatenate([_gelu_tanh(ybuf[j]) for j in range(ybuf.shape[0])], axis=1)
    gate = jax.nn.sigmoid(jnp.dot(yg.astype(BF16), wglu_ref[...], preferred_element_type=F32) + bglu_ref[...])
    sn = _rms(yg * gate, gs_ref[...]).astype(BF16)
    an = _rms(a_ref[...].astype(F32), ga_ref[...]).astype(BF16)
    o_ref[...] = (x_ref[...] + jnp.dot(an, woa_ref[...], preferred_element_type=F32)
                  + jnp.dot(sn, wos_ref[...], preferred_element_type=F32))


def _mix(x, a, yg, wglu, bglu, ga, gs, woa, wos):
    n, d = x.shape
    tq = _tile(n, MIX_TILES)
    n_g, _, kw = yg.shape
    row = lambda i: (i, 0)
    return pl.pallas_call(
        _mix_kernel,
        grid=(n // tq,),
        in_specs=[pl.BlockSpec((tq, d), row), pl.BlockSpec((tq, a.shape[1]), row),
                  pl.BlockSpec((n_g, tq // CHUNK, kw), lambda i: (0, i, 0)),
                  _resident(wglu), _resident(bglu), _resident(ga), _resident(gs), _resident(woa), _resident(wos)],
        out_specs=pl.BlockSpec((tq, d), row),
        out_shape=jax.ShapeDtypeStruct((n, d), F32),
        scratch_shapes=[pltpu.VMEM((n_g * SSM_GROUP // LANES, tq, LANES), F32)],
        input_output_aliases={0: 0},
        compiler_params=_params("arbitrary"),
        name="mix",
    )(x, a, yg, wglu, bglu, ga, gs, woa, wos)


def _mlp_kernel(x_ref, g_ref, w1_ref, w2_ref, o_ref, h_ref):
    @pl.when(pl.program_id(1) == 0)
    def _():
        x = x_ref[...]
        h_ref[...] = _rms(x, g_ref[...]).astype(BF16)
        o_ref[...] = x

    t = jnp.dot(h_ref[...], w1_ref[...], preferred_element_type=F32)
    t = jnp.square(jnp.maximum(t, 0.0)).astype(BF16)
    o_ref[...] += jnp.dot(t, w2_ref[...], preferred_element_type=F32)


def _mlp(x, g, w1, w2, tf=512):
    n, d = x.shape
    dff = w1.shape[1]
    tq = _tile(n, DENSE_TILES)
    row = lambda i, j: (i, 0)
    return pl.pallas_call(
        _mlp_kernel,
        grid=(n // tq, dff // tf),
        in_specs=[pl.BlockSpec((tq, d), row), _resident(g),
                  pl.BlockSpec((d, tf), lambda i, j: (0, j)), pl.BlockSpec((tf, d), lambda i, j: (j, 0))],
        out_specs=pl.BlockSpec((tq, d), row),
        out_shape=jax.ShapeDtypeStruct((n, d), F32),
        scratch_shapes=[pltpu.VMEM((tq, d), BF16)],
        input_output_aliases={0: 0},
        compiler_params=_params("arbitrary", "arbitrary"),
        name="mlp",
    )(x, g, w1, w2)


def _final_kernel(x_ref, g_ref, o_ref):
    o_ref[...] = _rms(x_ref[...], g_ref[...])


def _final_norm(x, g):
    n, d = x.shape
    tq = _tile(n, DENSE_TILES)
    row = lambda i: (i, 0)
    return pl.pallas_call(
        _final_kernel,
        grid=(n // tq,),
        in_specs=[pl.BlockSpec((tq, d), row), _resident(g)],
        out_specs=pl.BlockSpec((tq, d), row),
        out_shape=jax.ShapeDtypeStruct((n, d), F32),
        compiler_params=_params("arbitrary"),
        name="final_norm",
    )(x, g)


def _rope_table(b, lt, lp):
    half = ROT_DIM // 2
    pos = jnp.arange(lp, dtype=F32)
    inv = ROPE_THETA ** (-jnp.arange(half, dtype=F32) * 2.0 / ROT_DIM)
    ang = pos[:, None] * inv[None, :]
    cos, sin = jnp.cos(ang), jnp.sin(ang)
    rest = HEAD_DIM - ROT_DIM
    c = jnp.concatenate([cos, cos, jnp.ones((lp, rest), F32)], axis=1)
    s1 = jnp.concatenate([-sin, jnp.zeros((lp, HEAD_DIM - half), F32)], axis=1)
    s2 = jnp.concatenate([jnp.zeros((lp, half), F32), sin, jnp.zeros((lp, rest), F32)], axis=1)
    valid = (jnp.arange(lp) < lt).astype(F32)[:, None]
    rep = LANES // HEAD_DIM
    tab = jnp.concatenate([jnp.tile(v * valid, (1, rep)) for v in (c, s1, s2)]
                          + [jnp.broadcast_to(valid, (lp, LANES))], axis=1)
    return jnp.tile(tab, (b, 1))


def _trunk(x_in, layers, meta_tokens, final_g):
    b, s, d = x_in.shape
    lt = N_META + s
    lp = _round_up(lt, ATTN_TILE)
    nc = lp // CHUNK
    meta = jnp.broadcast_to(meta_tokens[None].astype(F32), (b, N_META, d))
    x = jnp.concatenate([meta, x_in, jnp.zeros((b, lp - lt, d), F32)], axis=1).reshape(b * lp, d)
    tab = _rope_table(b, lt, lp)
    for p in layers:
        attn_w, kv_w = p["attn_w"], p["kv_w"]
        q, k, vt, xg = _inproj(x, p["g1"], p["w_in"], p["w_vt"], tab, attn_w, kv_w)
        a = _attention(q.reshape(b, lp, attn_w), k.reshape(b, lp, kv_w), vt, p["sink"], lt)
        yg = _ssm(xg, *p["ssm"], nc)
        x = _mix(x, a.reshape(b * lp, attn_w), yg, p["w_glu"], p["b_glu"], p["g_attn"], p["g_ssm"],
                 p["w_out_a"], p["w_out_s"])
        x = _mlp(x, p["g2"], p["w1"], p["w2"])
    x = _final_norm(x, final_g)
    return x.reshape(b, lp, d)[:, N_META:lt]


def kernel(x_prompt, x_sample, meta_tokens, norm1_g, w_in, attn_sink, ssm_a_re, ssm_a_im, ssm_log_dt, ssm_b_re, ssm_b_im, ssm_c_re, ssm_c_im, ssm_d, w_glu, b_glu, norm_attn_out, norm_ssm_out, w_out, norm2_g, w_mlp_in, w_mlp_out, final_norm_g):
    depth, d, in_w = w_in.shape
    attn_w = norm_attn_out.shape[1]
    ssm_w = norm_ssm_out.shape[1]
    kv_w = (in_w - attn_w - ssm_w) // 2
    nc_max = max(_round_up(N_META + xs.shape[1], ATTN_TILE) // CHUNK for xs in (x_prompt, x_sample))
    ssm = _ssm_prep(ssm_a_re, ssm_a_im, ssm_log_dt, ssm_b_re, ssm_b_im, ssm_c_re, ssm_c_im, ssm_d,
                    max(1, math.ceil(math.log2(nc_max))))
    row = lambda v: v.astype(F32)[None]
    layers = []
    for l in range(depth):
        layers.append(dict(
            attn_w=attn_w, kv_w=kv_w, ssm=tuple(v[l] for v in ssm),
            g1=row(norm1_g[l]), w_in=w_in[l].astype(BF16), sink=attn_sink[l].astype(F32),
            w_vt=w_in[l, :, attn_w + kv_w:attn_w + 2 * kv_w].T.astype(BF16),
            w_glu=w_glu[l].astype(BF16), b_glu=row(b_glu[l]),
            g_attn=row(norm_attn_out[l]), g_ssm=row(norm_ssm_out[l]),
            w_out_a=w_out[l, :attn_w].astype(BF16), w_out_s=w_out[l, attn_w:].astype(BF16),
            g2=row(norm2_g[l]), w1=w_mlp_in[l].astype(BF16), w2=w_mlp_out[l].astype(BF16)))
    fg = row(final_norm_g)
    return (_trunk(x_prompt, layers, meta_tokens, fg), _trunk(x_sample, layers, meta_tokens, fg))
```

```python
import functools
import math

import jax
import jax.numpy as jnp
from jax import lax
from jax.experimental import pallas as pl
from jax.experimental.pallas import tpu as pltpu

F32 = jnp.float32
BF16 = jnp.bfloat16

LANES = 128
BF16_ROWS = 16
N_META = 16
HEAD_DIM = 64
KV_REP = 4
ROT_DIM = HEAD_DIM // 4
ROPE_THETA = 500000.0
WINDOW = 128
ATTN_TILE = 128
SSM_GROUP = 16
CHUNK = 16
BLOCKS_PER_VREG = LANES // SSM_GROUP
NORM_EPS = 1e-6
NEG_INF = -1e30
LOG2E = math.log2(math.e)
VMEM_LIMIT = 56 * 1024 * 1024
DENSE_TILES = (768, 384, 256, 128)
MIX_TILES = (384, 256, 128)
FINAL_TILES = (1024, 512, 256, 128)
N_CHUNK_COLS = 512


def _round_up(n, m):
    return (n + m - 1) // m * m


def _tile(n, candidates):
    for t in candidates:
        if n % t == 0:
            return t
    raise ValueError(f"{n} rows cannot be tiled by any of {candidates}")


def _rms(x, g):
    return x * lax.rsqrt(jnp.mean(x * x, axis=-1, keepdims=True) + NORM_EPS) * g


def _params(*sem):
    return pltpu.CompilerParams(dimension_semantics=sem, vmem_limit_bytes=VMEM_LIMIT)


def _resident(arr):
    nd = arr.ndim
    return pl.BlockSpec(arr.shape, lambda *_: (0,) * nd, pipeline_mode=pl.Buffered(1))


def _block_transpose(parts):
    blk = lax.broadcasted_iota(jnp.int32, parts[0].shape, 1) // SSM_GROUP
    b = BLOCKS_PER_VREG // 2
    while b:
        upper = (blk & b) != 0
        nxt = list(parts)
        for i in range(len(parts)):
            if i & b == 0:
                lo, hi = parts[i], parts[i + b]
                nxt[i] = jnp.where(upper, pltpu.roll(hi, SSM_GROUP * b, 1), lo)
                nxt[i + b] = jnp.where(upper, hi, pltpu.roll(lo, LANES - SSM_GROUP * b, 1))
        parts = nxt
        b //= 2
    return parts


def _inproj_kernel(x_ref, g_ref, w_ref, wvt_ref, tab_ref, q_ref, k_ref, vt_ref, xg_ref, ubuf, *, attn_w, kv_w):
    tq, d = x_ref.shape
    cos, s1, s2 = tab_ref[:, 0:LANES], tab_ref[:, LANES:2 * LANES], tab_ref[:, 2 * LANES:3 * LANES]
    valid = tab_ref[:, 3 * LANES:4 * LANES] > 0.0
    h = jnp.where(jnp.concatenate([valid] * (d // LANES), axis=1), _rms(x_ref[...], g_ref[...]), 0.0).astype(BF16)
    half = ROT_DIM // 2
    scale = HEAD_DIM ** -0.5 * LOG2E

    def rope(blk):
        return blk * cos + pltpu.roll(blk, LANES - half, 1) * s1 + pltpu.roll(blk, half, 1) * s2

    u0 = attn_w + 2 * kv_w
    spans = [(c, min(N_CHUNK_COLS, attn_w + kv_w - c)) for c in range(0, attn_w + kv_w, N_CHUNK_COLS)]
    spans += [(c, min(N_CHUNK_COLS, w_ref.shape[1] - c)) for c in range(u0, w_ref.shape[1], N_CHUNK_COLS)]
    for c0, width in spans:
        z = jnp.dot(h, w_ref[:, c0:c0 + width], preferred_element_type=F32)
        for j in range(width // LANES):
            col = c0 + j * LANES
            blk = z[:, j * LANES:(j + 1) * LANES]
            if col < attn_w:
                q_ref[:, col:col + LANES] = (rope(blk) * scale).astype(BF16)
            elif col < attn_w + kv_w:
                k_ref[:, col - attn_w:col - attn_w + LANES] = rope(blk).astype(BF16)
            else:
                ubuf[(col - u0) // LANES] = blk
    vt_ref[...] = lax.dot_general(wvt_ref[...], h, (((1,), (1,)), ((), ())),
                                  preferred_element_type=F32).astype(BF16)

    nch = tq // CHUNK
    per = 2 * BLOCKS_PER_VREG
    for gb in range(ubuf.shape[0] // 2):
        lo = [ubuf[2 * gb, pl.ds(s, nch, stride=CHUNK), :] for s in range(CHUNK)]
        hi = [ubuf[2 * gb + 1, pl.ds(s, nch, stride=CHUNK), :] for s in range(CHUNK)]
        a, b = _block_transpose(lo[:BLOCKS_PER_VREG]), _block_transpose(lo[BLOCKS_PER_VREG:])
        c, d = _block_transpose(hi[:BLOCKS_PER_VREG]), _block_transpose(hi[BLOCKS_PER_VREG:])
        for i in range(BLOCKS_PER_VREG):
            xg_ref[gb * per + i, :, 0:LANES] = a[i]
            xg_ref[gb * per + i, :, LANES:2 * LANES] = b[i]
            xg_ref[gb * per + BLOCKS_PER_VREG + i, :, 0:LANES] = c[i]
            xg_ref[gb * per + BLOCKS_PER_VREG + i, :, LANES:2 * LANES] = d[i]


def _inproj(x, g, w, wvt, tab, attn_w, kv_w):
    n, d = x.shape
    tq = _tile(n, DENSE_TILES)
    ssm_w = w.shape[1] - attn_w - 2 * kv_w
    n_g = ssm_w // SSM_GROUP
    row = lambda i: (i, 0)
    return pl.pallas_call(
        functools.partial(_inproj_kernel, attn_w=attn_w, kv_w=kv_w),
        grid=(n // tq,),
        in_specs=[pl.BlockSpec((tq, d), row), _resident(g), _resident(w), _resident(wvt),
                  pl.BlockSpec((tq, 4 * LANES), row)],
        out_specs=[pl.BlockSpec((tq, attn_w), row), pl.BlockSpec((tq, kv_w), row),
                   pl.BlockSpec((kv_w, tq), lambda i: (0, i)),
                   pl.BlockSpec((n_g, tq // CHUNK, CHUNK * SSM_GROUP), lambda i: (0, i, 0))],
        out_shape=[jax.ShapeDtypeStruct((n, attn_w), BF16), jax.ShapeDtypeStruct((n, kv_w), BF16),
                   jax.ShapeDtypeStruct((kv_w, n), BF16),
                   jax.ShapeDtypeStruct((n_g, n // CHUNK, CHUNK * SSM_GROUP), F32)],
        scratch_shapes=[pltpu.VMEM((ssm_w // LANES, tq, LANES), F32)],
        compiler_params=_params("arbitrary"),
        name="inproj",
    )(x, g, w, wvt, tab)


def _attn_kernel(sink_ref, q_ref, kp_ref, kc_ref, kn_ref, k0_ref, vp_ref, vc_ref, vn_ref, v0_ref, o_ref, *,
                 lt, n_heads):
    a = pl.program_id(1)
    t = ATTN_TILE
    q = q_ref[0]
    kall = jnp.concatenate([kp_ref[0], kc_ref[0], kn_ref[0], k0_ref[0]], axis=0)
    vtall = jnp.concatenate([vp_ref[...], vc_ref[...], vn_ref[...], v0_ref[...]], axis=1)
    qpos = a * t + lax.broadcasted_iota(jnp.int32, (3 * t, t), 1)
    kpos = (a - 1) * t + lax.broadcasted_iota(jnp.int32, (3 * t, t), 0)
    band = (jnp.abs(qpos - kpos) <= WINDOW) & (kpos >= 0) & (kpos < lt)
    mpos = lax.broadcasted_iota(jnp.int32, (t, t), 0)
    far_meta = (mpos < N_META) & (a * t + lax.broadcasted_iota(jnp.int32, (t, t), 1) - mpos > WINDOW)
    bias = jnp.where(jnp.concatenate([band, far_meta], axis=0), 0.0, NEG_INF)
    bias = jnp.concatenate([bias] * KV_REP, axis=1)
    nt = (((1,), (1,)), ((), ()))
    ones = jnp.ones((BF16_ROWS, 4 * t), BF16)
    for g in range(n_heads // KV_REP):
        heads = range(g * KV_REP, (g + 1) * KV_REP)
        qg = jnp.concatenate([q[:, h * HEAD_DIM:(h + 1) * HEAD_DIM] for h in heads], axis=0)
        ks = slice(g * HEAD_DIM, (g + 1) * HEAD_DIM)
        s = lax.dot_general(kall[:, ks], qg, nt, preferred_element_type=F32) + bias
        sink = jnp.concatenate([jnp.full((1, t), sink_ref[h] * LOG2E, F32) for h in heads], axis=1)
        m = jnp.maximum(jnp.max(s, axis=0, keepdims=True), sink)
        e = jnp.exp2(s - m).astype(BF16)
        oa = jnp.dot(jnp.concatenate([vtall[ks, :], ones], axis=0), e, preferred_element_type=F32)
        den = oa[HEAD_DIM:HEAD_DIM + 1] + jnp.exp2(sink - m)
        o = oa[:HEAD_DIM] * (1.0 / den)
        for r in range(0, KV_REP, 2):
            pair = jnp.concatenate([o[:, r * t:(r + 1) * t], o[:, (r + 1) * t:(r + 2) * t]], axis=0)
            lo = (g * KV_REP + r) * HEAD_DIM
            o_ref[0, :, lo:lo + 2 * HEAD_DIM] = pair.T.astype(BF16)


def _attention(q, k, vt, sink, lt):
    b, lp, attn_w = q.shape
    kv_w = k.shape[2]
    t = ATTN_TILE
    nt = lp // t
    cur = lambda bi, a: (bi, a, 0)
    prev = lambda bi, a: (bi, jnp.maximum(a - 1, 0), 0)
    nxt = lambda bi, a: (bi, jnp.minimum(a + 1, nt - 1), 0)
    first = lambda bi, a: (bi, 0, 0)
    kspec = lambda m: pl.BlockSpec((1, t, kv_w), m)
    vspec = lambda m: pl.BlockSpec((kv_w, t), lambda bi, a: (0, bi * nt + m(bi, a)[1]))
    return pl.pallas_call(
        functools.partial(_attn_kernel, lt=lt, n_heads=attn_w // HEAD_DIM),
        grid=(b, nt),
        in_specs=[pl.BlockSpec(memory_space=pltpu.SMEM), pl.BlockSpec((1, t, attn_w), cur),
                  kspec(prev), kspec(cur), kspec(nxt), kspec(first),
                  vspec(prev), vspec(cur), vspec(nxt), vspec(first)],
        out_specs=pl.BlockSpec((1, t, attn_w), cur),
        out_shape=jax.ShapeDtypeStruct((b, lp, attn_w), BF16),
        compiler_params=_params("arbitrary", "arbitrary"),
        name="attention",
    )(sink, q, k, k, k, k, vt, vt, vt, vt)


def _ssm_prep(a_re, a_im, log_dt, b_re, b_im, c_re, c_im, d, nsteps):
    hp = lax.Precision.HIGHEST
    t = CHUNK
    a_re, a_im, b_re, b_im = (v.astype(F32) for v in (a_re, a_im, b_re, b_im))
    n_l, _, n_g, n_p, n_h = b_re.shape
    dt = jnp.exp(log_dt.astype(F32))[..., None]

    def cpow(j):
        j = j.astype(F32)
        mag = jnp.exp((a_re * dt)[..., None] * j)
        ang = (a_im * dt)[..., None] * j
        return mag * jnp.cos(ang), mag * jnp.sin(ang)

    mag = jnp.exp(a_re * dt)
    ang = a_im * dt
    abr, abi = mag * jnp.cos(ang), mag * jnp.sin(ang)
    den = a_re * a_re + a_im * a_im
    nr = abr - 1.0
    cr = (nr * a_re + abi * a_im) / den
    ci = (abi * a_re - nr * a_im) / den
    bbr = cr[..., None] * b_re - ci[..., None] * b_im
    bbi = cr[..., None] * b_im + ci[..., None] * b_re

    pr, pi = cpow(jnp.arange(t + 1))
    c_ret = jnp.swapaxes(c_re.astype(F32), -1, -2)
    c_imt = jnp.swapaxes(c_im.astype(F32), -1, -2)
    car = c_ret[..., None, :] * pr[..., None] - c_imt[..., None, :] * pi[..., None]
    cai = c_ret[..., None, :] * pi[..., None] + c_imt[..., None, :] * pr[..., None]
    kj = (jnp.einsum('ldgpjh,ldgpk->ldgkjh', car[..., :t, :], bbr, precision=hp)
          - jnp.einsum('ldgpjh,ldgpk->ldgkjh', cai[..., :t, :], bbi, precision=hp))
    kf, kb = kj[:, 0], kj[:, 1]
    dmat = jnp.eye(n_h, dtype=F32)[:, None, :] * d.astype(F32).reshape(n_l, n_g, 1, 1, n_h)
    centre = kf[..., 0:1, :] + kb[..., 0:1, :] + dmat
    kall = jnp.concatenate([kb[..., :0:-1, :], centre, kf[..., 1:, :]], axis=-2)
    mt = jnp.stack([kall[..., t - 1 - s:2 * t - 1 - s, :] for s in range(t)], axis=2)
    mt = mt.reshape(n_l, n_g, t * n_h, t * n_h)

    def b_pow(pw_r, pw_i, d_idx):
        pw_r, pw_i = (jnp.swapaxes(v, -1, -2)[..., None, :] for v in (pw_r, pw_i))
        br, bi = (jnp.swapaxes(v[:, d_idx], -1, -2)[:, :, None] for v in (bbr, bbi))
        shape = (n_l, n_g, t * n_h, n_p)
        return (pw_r * br - pw_i * bi).reshape(shape), (pw_r * bi + pw_i * br).reshape(shape)

    wf_r, wf_i = b_pow(pr[:, 0, ..., t - 1::-1], pi[:, 0, ..., t - 1::-1], 0)
    wb_r, wb_i = b_pow(pr[:, 1, ..., :t], pi[:, 1, ..., :t], 1)
    ws = jnp.concatenate([wf_r, wb_r, wf_i, wb_i], axis=-1)

    flat = lambda v: v.reshape(n_l, n_g, n_p, t * n_h)
    wo = jnp.concatenate([flat(car[:, 0, ..., 1:, :]), flat(car[:, 1, ..., t:0:-1, :]),
                          flat(-cai[:, 0, ..., 1:, :]), flat(-cai[:, 1, ..., t:0:-1, :])], axis=2)

    sr, si = cpow(t * (2 ** jnp.arange(nsteps)))
    lanes = lambda v: jnp.concatenate([v[:, 0], v[:, 1]], axis=2)
    pw = jnp.stack([lanes(sr), lanes(si)], axis=-1)
    pw = jnp.transpose(pw, (0, 1, 3, 4, 2)).reshape(n_l, n_g, 2 * nsteps, 2 * n_p)
    return mt.astype(BF16), ws.astype(BF16), wo.astype(BF16), pw


def _ssm_kernel(x_ref, mt_ref, ws_ref, wo_ref, pw_ref, y_ref, rbuf, ibuf, *, nc, nsteps, pad):
    x = x_ref[0].astype(BF16)
    s = jnp.dot(x, ws_ref[0], preferred_element_type=F32)
    hw = s.shape[1] // 2
    zeros = jnp.zeros((pad, hw), F32)
    for buf, part in ((rbuf, s[:, :hw]), (ibuf, s[:, hw:])):
        buf[0:pad, :] = zeros
        buf[pad + nc:pad + nc + pad, :] = zeros
        buf[pad:pad + nc, :] = part
    fwd = lax.broadcasted_iota(jnp.int32, (nc, hw), 1) < hw // 2

    def shifted(buf, d):
        return jnp.where(fwd, buf[pad - d:pad - d + nc, :], buf[pad + d:pad + d + nc, :])

    d = 1
    for k in range(nsteps):
        ar, ai = pw_ref[0, 2 * k:2 * k + 1, :], pw_ref[0, 2 * k + 1:2 * k + 2, :]
        sr, si = shifted(rbuf, d), shifted(ibuf, d)
        new_r = rbuf[pad:pad + nc, :] + ar * sr - ai * si
        new_i = ibuf[pad:pad + nc, :] + ar * si + ai * sr
        rbuf[pad:pad + nc, :] = new_r
        ibuf[pad:pad + nc, :] = new_i
        d *= 2
    hin = jnp.concatenate([shifted(rbuf, 1), shifted(ibuf, 1)], axis=1).astype(BF16)
    y_ref[0] = (jnp.dot(x, mt_ref[0], preferred_element_type=F32)
                + jnp.dot(hin, wo_ref[0], preferred_element_type=F32))


def _ssm(xg, mt, ws, wo, pw, nc):
    n_g, rows, kw = xg.shape
    nsteps = max(1, math.ceil(math.log2(nc)))
    pad = max(8, 2 ** (nsteps - 1))
    grp = lambda bi, g: (g, 0, 0)
    io = lambda bi, g: (g, bi, 0)
    hw = ws.shape[2] // 2
    return pl.pallas_call(
        functools.partial(_ssm_kernel, nc=nc, nsteps=nsteps, pad=pad),
        grid=(rows // nc, n_g),
        in_specs=[pl.BlockSpec((1, nc, kw), io), pl.BlockSpec((1,) + mt.shape[1:], grp),
                  pl.BlockSpec((1,) + ws.shape[1:], grp), pl.BlockSpec((1,) + wo.shape[1:], grp),
                  pl.BlockSpec((1,) + pw.shape[1:], grp)],
        out_specs=pl.BlockSpec((1, nc, kw), io),
        out_shape=jax.ShapeDtypeStruct(xg.shape, F32),
        scratch_shapes=[pltpu.VMEM((nc + 2 * pad, hw), F32), pltpu.VMEM((nc + 2 * pad, hw), F32)],
        compiler_params=_params("arbitrary", "arbitrary"),
        name="ssm",
    )(xg, mt, ws, wo, pw)


def _gelu_tanh(x):
    return 0.5 * x * (1.0 + jnp.tanh(math.sqrt(2.0 / math.pi) * (x + 0.044715 * (x * x * x))))


def _mix_kernel(x_ref, a_ref, yg_ref, wglu_ref, bglu_ref, ga_ref, gs_ref, woa_ref, wos_ref, o_ref, ybuf):
    nch = yg_ref.shape[1]
    per = 2 * BLOCKS_PER_VREG
    for gb in range(ybuf.shape[0] // 2):
        grp = lambda i, half: yg_ref[gb * per + i, :, half * LANES:(half + 1) * LANES]
        a = _block_transpose([grp(i, 0) for i in range(BLOCKS_PER_VREG)])
        b = _block_transpose([grp(i, 1) for i in range(BLOCKS_PER_VREG)])
        c = _block_transpose([grp(BLOCKS_PER_VREG + i, 0) for i in range(BLOCKS_PER_VREG)])
        d = _block_transpose([grp(BLOCKS_PER_VREG + i, 1) for i in range(BLOCKS_PER_VREG)])
        for s in range(BLOCKS_PER_VREG):
            ybuf[2 * gb, pl.ds(s, nch, stride=CHUNK), :] = a[s]
            ybuf[2 * gb, pl.ds(BLOCKS_PER_VREG + s, nch, stride=CHUNK), :] = b[s]
            ybuf[2 * gb + 1, pl.ds(s, nch, stride=CHUNK), :] = c[s]
            ybuf[2 * gb + 1, pl.ds(BLOCKS_PER_VREG + s, nch, stride=CHUNK), :] = d[s]
    yg = jnp.concatenate([_gelu_tanh(ybuf[j]) for j in range(ybuf.shape[0])], axis=1)
    gate = jax.nn.sigmoid(jnp.dot(yg.astype(BF16), wglu_ref[...], preferred_element_type=F32) + bglu_ref[...])
    sn = _rms(yg * gate, gs_ref[...]).astype(BF16)
    an = _rms(a_ref[...].astype(F32), ga_ref[...]).astype(BF16)
    o_ref[...] = (x_ref[...] + jnp.dot(an, woa_ref[...], preferred_element_type=F32)
                  + jnp.dot(sn, wos_ref[...], preferred_element_type=F32))


def _mix(x, a, yg, wglu, bglu, ga, gs, woa, wos):
    n, d = x.shape
    tq = _tile(n, MIX_TILES)
    n_g, _, kw = yg.shape
    row = lambda i: (i, 0)
    return pl.pallas_call(
        _mix_kernel,
        grid=(n // tq,),
        in_specs=[pl.BlockSpec((tq, d), row), pl.BlockSpec((tq, a.shape[1]), row),
                  pl.BlockSpec((n_g, tq // CHUNK, kw), lambda i: (0, i, 0)),
                  _resident(wglu), _resident(bglu), _resident(ga), _resident(gs), _resident(woa), _resident(wos)],
        out_specs=pl.BlockSpec((tq, d), row),
        out_shape=jax.ShapeDtypeStruct((n, d), F32),
        scratch_shapes=[pltpu.VMEM((n_g * SSM_GROUP // LANES, tq, LANES), F32)],
        input_output_aliases={0: 0},
        compiler_params=_params("arbitrary"),
        name="mix",
    )(x, a, yg, wglu, bglu, ga, gs, woa, wos)


def _mlp_kernel(x_ref, g_ref, w1_ref, w2_ref, o_ref, h_ref):
    @pl.when(pl.program_id(1) == 0)
    def _():
        x = x_ref[...]
        h_ref[...] = _rms(x, g_ref[...]).astype(BF16)
        o_ref[...] = x

    t = jnp.dot(h_ref[...], w1_ref[...], preferred_element_type=F32)
    t = jnp.square(jnp.maximum(t, 0.0)).astype(BF16)
    o_ref[...] += jnp.dot(t, w2_ref[...], preferred_element_type=F32)


def _mlp(x, g, w1, w2, tf=512):
    n, d = x.shape
    dff = w1.shape[1]
    tq = _tile(n, DENSE_TILES)
    row = lambda i, j: (i, 0)
    return pl.pallas_call(
        _mlp_kernel,
        grid=(n // tq, dff // tf),
        in_specs=[pl.BlockSpec((tq, d), row), _resident(g),
                  pl.BlockSpec((d, tf), lambda i, j: (0, j)), pl.BlockSpec((tf, d), lambda i, j: (j, 0))],
        out_specs=pl.BlockSpec((tq, d), row),
        out_shape=jax.ShapeDtypeStruct((n, d), F32),
        scratch_shapes=[pltpu.VMEM((tq, d), BF16)],
        input_output_aliases={0: 0},
        compiler_params=_params("arbitrary", "arbitrary"),
        name="mlp",
    )(x, g, w1, w2)


def _final_kernel(x_ref, g_ref, o_ref):
    o_ref[0] = _rms(x_ref[0], g_ref[...])


def _final_norm(x, g, s):
    b, _, d = x.shape
    tq = _tile(s, FINAL_TILES)
    return pl.pallas_call(
        _final_kernel,
        grid=(b, s // tq),
        in_specs=[pl.BlockSpec((pl.Element(1), pl.Element(tq), pl.Element(d)),
                               lambda bi, i: (bi, pl.multiple_of(N_META + i * tq, N_META), 0)), _resident(g)],
        out_specs=pl.BlockSpec((1, tq, d), lambda bi, i: (bi, i, 0)),
        out_shape=jax.ShapeDtypeStruct((b, s, d), F32),
        compiler_params=_params("arbitrary", "arbitrary"),
        name="final_norm",
    )(x, g)


def _rope_table(b, lt, lp):
    half = ROT_DIM // 2
    pos = jnp.arange(lp, dtype=F32)
    inv = ROPE_THETA ** (-jnp.arange(half, dtype=F32) * 2.0 / ROT_DIM)
    ang = pos[:, None] * inv[None, :]
    cos, sin = jnp.cos(ang), jnp.sin(ang)
    rest = HEAD_DIM - ROT_DIM
    c = jnp.concatenate([cos, cos, jnp.ones((lp, rest), F32)], axis=1)
    s1 = jnp.concatenate([-sin, jnp.zeros((lp, HEAD_DIM - half), F32)], axis=1)
    s2 = jnp.concatenate([jnp.zeros((lp, half), F32), sin, jnp.zeros((lp, rest), F32)], axis=1)
    valid = (jnp.arange(lp) < lt).astype(F32)[:, None]
    rep = LANES // HEAD_DIM
    tab = jnp.concatenate([jnp.tile(v * valid, (1, rep)) for v in (c, s1, s2)]
                          + [jnp.broadcast_to(valid, (lp, LANES))], axis=1)
    return jnp.tile(tab, (b, 1))


def _trunk(x_in, layers, meta_tokens, final_g):
    b, s, d = x_in.shape
    lt = N_META + s
    lp = _round_up(lt, ATTN_TILE)
    nc = lp // CHUNK
    meta = jnp.broadcast_to(meta_tokens[None].astype(F32), (b, N_META, d))
    x = jnp.concatenate([meta, x_in, jnp.zeros((b, lp - lt, d), F32)], axis=1).reshape(b * lp, d)
    tab = _rope_table(b, lt, lp)
    for p in layers:
        attn_w, kv_w = p["attn_w"], p["kv_w"]
        q, k, vt, xg = _inproj(x, p["g1"], p["w_in"], p["w_vt"], tab, attn_w, kv_w)
        a = _attention(q.reshape(b, lp, attn_w), k.reshape(b, lp, kv_w), vt, p["sink"], lt)
        yg = _ssm(xg, *p["ssm"], nc)
        x = _mix(x, a.reshape(b * lp, attn_w), yg, p["w_glu"], p["b_glu"], p["g_attn"], p["g_ssm"],
                 p["w_out_a"], p["w_out_s"])
        x = _mlp(x, p["g2"], p["w1"], p["w2"])
    return _final_norm(x.reshape(b, lp, d), final_g, s)


def kernel(x_prompt, x_sample, meta_tokens, norm1_g, w_in, attn_sink, ssm_a_re, ssm_a_im, ssm_log_dt, ssm_b_re, ssm_b_im, ssm_c_re, ssm_c_im, ssm_d, w_glu, b_glu, norm_attn_out, norm_ssm_out, w_out, norm2_g, w_mlp_in, w_mlp_out, final_norm_g):
    depth, d, in_w = w_in.shape
    attn_w = norm_attn_out.shape[1]
    ssm_w = norm_ssm_out.shape[1]
    kv_w = (in_w - attn_w - ssm_w) // 2
    nc_max = max(_round_up(N_META + xs.shape[1], ATTN_TILE) // CHUNK for xs in (x_prompt, x_sample))
    ssm = _ssm_prep(ssm_a_re, ssm_a_im, ssm_log_dt, ssm_b_re, ssm_b_im, ssm_c_re, ssm_c_im, ssm_d,
                    max(1, math.ceil(math.log2(nc_max))))
    row = lambda v: v.astype(F32)[None]
    layers = []
    for l in range(depth):
        layers.append(dict(
            attn_w=attn_w, kv_w=kv_w, ssm=tuple(v[l] for v in ssm),
            g1=row(norm1_g[l]), w_in=w_in[l].astype(BF16), sink=attn_sink[l].astype(F32),
            w_vt=w_in[l, :, attn_w + kv_w:attn_w + 2 * kv_w].T.astype(BF16),
            w_glu=w_glu[l].astype(BF16), b_glu=row(b_glu[l]),
            g_attn=row(norm_attn_out[l]), g_ssm=row(norm_ssm_out[l]),
            w_out_a=w_out[l, :attn_w].astype(BF16), w_out_s=w_out[l, attn_w:].astype(BF16),
            g2=row(norm2_g[l]), w1=w_mlp_in[l].astype(BF16), w2=w_mlp_out[l].astype(BF16)))
    fg = row(final_norm_g)
    return (_trunk(x_prompt, layers, meta_tokens, fg), _trunk(x_sample, layers, meta_tokens, fg))
```

```python
import functools
import math

import jax
import jax.numpy as jnp
from jax import lax
from jax.experimental import pallas as pl
from jax.experimental.pallas import tpu as pltpu

F32 = jnp.float32
BF16 = jnp.bfloat16

LANES = 128
BF16_ROWS = 16
N_META = 16
HEAD_DIM = 64
KV_REP = 4
ROT_DIM = HEAD_DIM // 4
ROPE_THETA = 500000.0
WINDOW = 128
ATTN_TILE = 128
SSM_GROUP = 16
CHUNK = 16
BLOCKS_PER_VREG = LANES // SSM_GROUP
NORM_EPS = 1e-6
NEG_INF = -1e30
LOG2E = math.log2(math.e)
VMEM_LIMIT = 56 * 1024 * 1024
DENSE_TILES = (768, 384, 256, 128)
MIX_TILES = (384, 256, 128)
FINAL_TILES = (1024, 512, 256, 128)
N_CHUNK_COLS = 512
SCAN_ROWS = 128


def _round_up(n, m):
    return (n + m - 1) // m * m


def _tile(n, candidates):
    for t in candidates:
        if n % t == 0:
            return t
    raise ValueError(f"{n} rows cannot be tiled by any of {candidates}")


def _rms(x, g):
    return x * lax.rsqrt(jnp.mean(x * x, axis=-1, keepdims=True) + NORM_EPS) * g


def _params(*sem):
    return pltpu.CompilerParams(dimension_semantics=sem, vmem_limit_bytes=VMEM_LIMIT)


def _resident(arr):
    nd = arr.ndim
    return pl.BlockSpec(arr.shape, lambda *_: (0,) * nd, pipeline_mode=pl.Buffered(1))


def _block_transpose(parts):
    blk = lax.broadcasted_iota(jnp.int32, parts[0].shape, 1) // SSM_GROUP
    b = BLOCKS_PER_VREG // 2
    while b:
        upper = (blk & b) != 0
        nxt = list(parts)
        for i in range(len(parts)):
            if i & b == 0:
                lo, hi = parts[i], parts[i + b]
                nxt[i] = jnp.where(upper, pltpu.roll(hi, SSM_GROUP * b, 1), lo)
                nxt[i + b] = jnp.where(upper, hi, pltpu.roll(lo, LANES - SSM_GROUP * b, 1))
        parts = nxt
        b //= 2
    return parts


def _inproj_kernel(x_ref, g_ref, w_ref, wvt_ref, tab_ref, q_ref, k_ref, vt_ref, xg_ref, ubuf, *, attn_w, kv_w):
    tq, d = x_ref.shape
    cos, s1, s2 = tab_ref[:, 0:LANES], tab_ref[:, LANES:2 * LANES], tab_ref[:, 2 * LANES:3 * LANES]
    valid = tab_ref[:, 3 * LANES:4 * LANES] > 0.0
    h = jnp.where(jnp.concatenate([valid] * (d // LANES), axis=1), _rms(x_ref[...], g_ref[...]), 0.0).astype(BF16)
    half = ROT_DIM // 2
    scale = HEAD_DIM ** -0.5 * LOG2E

    def rope(blk):
        return blk * cos + pltpu.roll(blk, LANES - half, 1) * s1 + pltpu.roll(blk, half, 1) * s2

    u0 = attn_w + 2 * kv_w
    spans = [(c, min(N_CHUNK_COLS, attn_w + kv_w - c)) for c in range(0, attn_w + kv_w, N_CHUNK_COLS)]
    spans += [(c, min(N_CHUNK_COLS, w_ref.shape[1] - c)) for c in range(u0, w_ref.shape[1], N_CHUNK_COLS)]
    for c0, width in spans:
        z = jnp.dot(h, w_ref[:, c0:c0 + width], preferred_element_type=F32)
        for j in range(width // LANES):
            col = c0 + j * LANES
            blk = z[:, j * LANES:(j + 1) * LANES]
            if col < attn_w:
                q_ref[:, col:col + LANES] = (rope(blk) * scale).astype(BF16)
            elif col < attn_w + kv_w:
                k_ref[:, col - attn_w:col - attn_w + LANES] = rope(blk).astype(BF16)
            else:
                ubuf[(col - u0) // LANES] = blk
    vt_ref[...] = lax.dot_general(wvt_ref[...], h, (((1,), (1,)), ((), ())),
                                  preferred_element_type=F32).astype(BF16)

    nch = tq // CHUNK
    per = 2 * BLOCKS_PER_VREG
    for gb in range(ubuf.shape[0] // 2):
        lo = [ubuf[2 * gb, pl.ds(s, nch, stride=CHUNK), :] for s in range(CHUNK)]
        hi = [ubuf[2 * gb + 1, pl.ds(s, nch, stride=CHUNK), :] for s in range(CHUNK)]
        a, b = _block_transpose(lo[:BLOCKS_PER_VREG]), _block_transpose(lo[BLOCKS_PER_VREG:])
        c, d = _block_transpose(hi[:BLOCKS_PER_VREG]), _block_transpose(hi[BLOCKS_PER_VREG:])
        for i in range(BLOCKS_PER_VREG):
            xg_ref[gb * per + i, :, 0:LANES] = a[i]
            xg_ref[gb * per + i, :, LANES:2 * LANES] = b[i]
            xg_ref[gb * per + BLOCKS_PER_VREG + i, :, 0:LANES] = c[i]
            xg_ref[gb * per + BLOCKS_PER_VREG + i, :, LANES:2 * LANES] = d[i]


def _inproj(x, g, w, wvt, tab, attn_w, kv_w):
    n, d = x.shape
    tq = _tile(n, DENSE_TILES)
    ssm_w = w.shape[1] - attn_w - 2 * kv_w
    n_g = ssm_w // SSM_GROUP
    row = lambda i: (i, 0)
    return pl.pallas_call(
        functools.partial(_inproj_kernel, attn_w=attn_w, kv_w=kv_w),
        grid=(n // tq,),
        in_specs=[pl.BlockSpec((tq, d), row), _resident(g), _resident(w), _resident(wvt),
                  pl.BlockSpec((tq, 4 * LANES), row)],
        out_specs=[pl.BlockSpec((tq, attn_w), row), pl.BlockSpec((tq, kv_w), row),
                   pl.BlockSpec((kv_w, tq), lambda i: (0, i)),
                   pl.BlockSpec((n_g, tq // CHUNK, CHUNK * SSM_GROUP), lambda i: (0, i, 0))],
        out_shape=[jax.ShapeDtypeStruct((n, attn_w), BF16), jax.ShapeDtypeStruct((n, kv_w), BF16),
                   jax.ShapeDtypeStruct((kv_w, n), BF16),
                   jax.ShapeDtypeStruct((n_g, n // CHUNK, CHUNK * SSM_GROUP), F32)],
        scratch_shapes=[pltpu.VMEM((ssm_w // LANES, tq, LANES), F32)],
        compiler_params=_params("arbitrary"),
        name="inproj",
    )(x, g, w, wvt, tab)


def _attn_kernel(sink_ref, q_ref, kp_ref, kc_ref, kn_ref, k0_ref, vp_ref, vc_ref, vn_ref, v0_ref, o_ref, *,
                 lt, n_heads):
    a = pl.program_id(1)
    t = ATTN_TILE
    q = q_ref[0]
    kall = jnp.concatenate([kp_ref[0], kc_ref[0], kn_ref[0], k0_ref[0]], axis=0)
    vtall = jnp.concatenate([vp_ref[...], vc_ref[...], vn_ref[...], v0_ref[...]], axis=1)
    qpos = a * t + lax.broadcasted_iota(jnp.int32, (3 * t, t), 1)
    kpos = (a - 1) * t + lax.broadcasted_iota(jnp.int32, (3 * t, t), 0)
    band = (jnp.abs(qpos - kpos) <= WINDOW) & (kpos >= 0) & (kpos < lt)
    mpos = lax.broadcasted_iota(jnp.int32, (t, t), 0)
    far_meta = (mpos < N_META) & (a * t + lax.broadcasted_iota(jnp.int32, (t, t), 1) - mpos > WINDOW)
    bias = jnp.where(jnp.concatenate([band, far_meta], axis=0), 0.0, NEG_INF)
    bias = jnp.concatenate([bias] * KV_REP, axis=1)
    nt = (((1,), (1,)), ((), ()))
    ones = jnp.ones((BF16_ROWS, 4 * t), BF16)
    for g in range(n_heads // KV_REP):
        heads = range(g * KV_REP, (g + 1) * KV_REP)
        qg = jnp.concatenate([q[:, h * HEAD_DIM:(h + 1) * HEAD_DIM] for h in heads], axis=0)
        ks = slice(g * HEAD_DIM, (g + 1) * HEAD_DIM)
        s = lax.dot_general(kall[:, ks], qg, nt, preferred_element_type=F32) + bias
        sink = jnp.concatenate([jnp.full((1, t), sink_ref[h] * LOG2E, F32) for h in heads], axis=1)
        m = jnp.maximum(jnp.max(s, axis=0, keepdims=True), sink)
        e = jnp.exp2(s - m).astype(BF16)
        oa = jnp.dot(jnp.concatenate([vtall[ks, :], ones], axis=0), e, preferred_element_type=F32)
        den = oa[HEAD_DIM:HEAD_DIM + 1] + jnp.exp2(sink - m)
        o = oa[:HEAD_DIM] * (1.0 / den)
        for r in range(0, KV_REP, 2):
            pair = jnp.concatenate([o[:, r * t:(r + 1) * t], o[:, (r + 1) * t:(r + 2) * t]], axis=0)
            lo = (g * KV_REP + r) * HEAD_DIM
            o_ref[0, :, lo:lo + 2 * HEAD_DIM] = pair.T.astype(BF16)


def _attention(q, k, vt, sink, lt):
    b, lp, attn_w = q.shape
    kv_w = k.shape[2]
    t = ATTN_TILE
    nt = lp // t
    cur = lambda bi, a: (bi, a, 0)
    prev = lambda bi, a: (bi, jnp.maximum(a - 1, 0), 0)
    nxt = lambda bi, a: (bi, jnp.minimum(a + 1, nt - 1), 0)
    first = lambda bi, a: (bi, 0, 0)
    kspec = lambda m: pl.BlockSpec((1, t, kv_w), m)
    vspec = lambda m: pl.BlockSpec((kv_w, t), lambda bi, a: (0, bi * nt + m(bi, a)[1]))
    return pl.pallas_call(
        functools.partial(_attn_kernel, lt=lt, n_heads=attn_w // HEAD_DIM),
        grid=(b, nt),
        in_specs=[pl.BlockSpec(memory_space=pltpu.SMEM), pl.BlockSpec((1, t, attn_w), cur),
                  kspec(prev), kspec(cur), kspec(nxt), kspec(first),
                  vspec(prev), vspec(cur), vspec(nxt), vspec(first)],
        out_specs=pl.BlockSpec((1, t, attn_w), cur),
        out_shape=jax.ShapeDtypeStruct((b, lp, attn_w), BF16),
        compiler_params=_params("arbitrary", "arbitrary"),
        name="attention",
    )(sink, q, k, k, k, k, vt, vt, vt, vt)


def _ssm_prep(a_re, a_im, log_dt, b_re, b_im, c_re, c_im, d, nsteps):
    hp = lax.Precision.HIGHEST
    t = CHUNK
    a_re, a_im, b_re, b_im = (v.astype(F32) for v in (a_re, a_im, b_re, b_im))
    n_l, _, n_g, n_p, n_h = b_re.shape
    dt = jnp.exp(log_dt.astype(F32))[..., None]

    def cpow(j):
        j = j.astype(F32)
        mag = jnp.exp((a_re * dt)[..., None] * j)
        ang = (a_im * dt)[..., None] * j
        return mag * jnp.cos(ang), mag * jnp.sin(ang)

    mag = jnp.exp(a_re * dt)
    ang = a_im * dt
    abr, abi = mag * jnp.cos(ang), mag * jnp.sin(ang)
    den = a_re * a_re + a_im * a_im
    nr = abr - 1.0
    cr = (nr * a_re + abi * a_im) / den
    ci = (abi * a_re - nr * a_im) / den
    bbr = cr[..., None] * b_re - ci[..., None] * b_im
    bbi = cr[..., None] * b_im + ci[..., None] * b_re

    pr0, pi0 = cpow(jnp.arange(t + 1))
    pr, pi = (jnp.repeat(v, n_h, axis=-1) for v in (pr0, pi0))
    c_ret = jnp.tile(jnp.swapaxes(c_re.astype(F32), -1, -2), (1, 1, 1, 1, t + 1))
    c_imt = jnp.tile(jnp.swapaxes(c_im.astype(F32), -1, -2), (1, 1, 1, 1, t + 1))
    car = c_ret * pr - c_imt * pi
    cai = c_ret * pi + c_imt * pr
    w = t * n_h
    kj = (jnp.einsum('ldgpx,ldgpk->ldgkx', car[..., :w], bbr, precision=hp)
          - jnp.einsum('ldgpx,ldgpk->ldgkx', cai[..., :w], bbi, precision=hp))
    kf, kb = kj[:, 0], kj[:, 1]
    dmat = jnp.eye(n_h, dtype=F32) * d.astype(F32).reshape(n_l, n_g, 1, n_h)
    centre = kf[..., :n_h] + kb[..., :n_h] + dmat
    kb_rev = jnp.flip(kb.reshape(n_l, n_g, n_h, t, n_h), axis=3).reshape(n_l, n_g, n_h, w)
    kall = jnp.concatenate([kb_rev[..., :w - n_h], centre, kf[..., n_h:]], axis=-1)
    mt = jnp.stack([kall[..., (t - 1 - s) * n_h:(2 * t - 1 - s) * n_h] for s in range(t)], axis=2)
    mt = mt.reshape(n_l, n_g, w, w)

    def b_pow(pw_r, pw_i, d_idx):
        pw_r, pw_i = (jnp.swapaxes(v, -1, -2)[..., None, :] for v in (pw_r, pw_i))
        br, bi = (jnp.swapaxes(v[:, d_idx], -1, -2)[:, :, None] for v in (bbr, bbi))
        shape = (n_l, n_g, t * n_h, n_p)
        return (pw_r * br - pw_i * bi).reshape(shape), (pw_r * bi + pw_i * br).reshape(shape)

    wf_r, wf_i = b_pow(pr0[:, 0, ..., t - 1::-1], pi0[:, 0, ..., t - 1::-1], 0)
    wb_r, wb_i = b_pow(pr0[:, 1, ..., :t], pi0[:, 1, ..., :t], 1)
    ws = jnp.concatenate([wf_r, wb_r, wf_i, wb_i], axis=-1)

    fwd_rows = lambda v: v[:, 0, ..., n_h:]
    bwd_rows = lambda v: jnp.flip(v[:, 1].reshape(n_l, n_g, n_p, t + 1, n_h), axis=3)[..., :t, :].reshape(
        n_l, n_g, n_p, w)
    wo = jnp.concatenate([fwd_rows(car), bwd_rows(car), -fwd_rows(cai), -bwd_rows(cai)], axis=2)

    sr, si = cpow(t * (2 ** jnp.arange(nsteps)))
    lanes = lambda v: jnp.concatenate([v[:, 0], v[:, 1]], axis=2)
    pw = jnp.stack([lanes(sr), lanes(si)], axis=-1)
    pw = jnp.transpose(pw, (0, 1, 3, 4, 2)).reshape(n_l, n_g, 2 * nsteps, 2 * n_p)
    return mt.astype(BF16), ws.astype(BF16), wo.astype(BF16), pw


def _ssm_kernel(x_ref, mt_ref, ws_ref, wo_ref, pw_ref, y_ref, rbuf, ibuf, *, nc, nsteps, pad):
    x = x_ref[0].astype(BF16)
    s = jnp.dot(x, ws_ref[0], preferred_element_type=F32)
    hw = s.shape[1] // 2
    zeros = jnp.zeros((pad, hw), F32)
    for slot in range(2):
        for buf, part in ((rbuf, s[:, :hw]), (ibuf, s[:, hw:])):
            buf[slot, 0:pad, :] = zeros
            buf[slot, pad + nc:pad + nc + pad, :] = zeros
            if slot == 0:
                buf[slot, pad:pad + nc, :] = part
    rows = [(r0, min(SCAN_ROWS, nc - r0)) for r0 in range(0, nc, SCAN_ROWS)]

    def shifted(buf, slot, d, r0, n):
        fwd = lax.broadcasted_iota(jnp.int32, (n, hw), 1) < hw // 2
        lo = pad + r0
        return jnp.where(fwd, buf[slot, lo - d:lo - d + n, :], buf[slot, lo + d:lo + d + n, :])

    d = 1
    for k in range(nsteps):
        src, dst = k % 2, (k + 1) % 2
        ar, ai = pw_ref[0, 2 * k:2 * k + 1, :], pw_ref[0, 2 * k + 1:2 * k + 2, :]
        for r0, n in rows:
            lo = pad + r0
            sr, si = shifted(rbuf, src, d, r0, n), shifted(ibuf, src, d, r0, n)
            rbuf[dst, lo:lo + n, :] = rbuf[src, lo:lo + n, :] + ar * sr - ai * si
            ibuf[dst, lo:lo + n, :] = ibuf[src, lo:lo + n, :] + ar * si + ai * sr
        d *= 2
    last = nsteps % 2
    hin = jnp.concatenate([shifted(rbuf, last, 1, 0, nc), shifted(ibuf, last, 1, 0, nc)],
                          axis=1).astype(BF16)
    y_ref[0] = (jnp.dot(x, mt_ref[0], preferred_element_type=F32)
                + jnp.dot(hin, wo_ref[0], preferred_element_type=F32))


def _ssm(xg, mt, ws, wo, pw, nc):
    n_g, rows, kw = xg.shape
    nsteps = max(1, math.ceil(math.log2(nc)))
    pad = max(8, 2 ** (nsteps - 1))
    grp = lambda bi, g: (g, 0, 0)
    io = lambda bi, g: (g, bi, 0)
    hw = ws.shape[2] // 2
    return pl.pallas_call(
        functools.partial(_ssm_kernel, nc=nc, nsteps=nsteps, pad=pad),
        grid=(rows // nc, n_g),
        in_specs=[pl.BlockSpec((1, nc, kw), io), pl.BlockSpec((1,) + mt.shape[1:], grp),
                  pl.BlockSpec((1,) + ws.shape[1:], grp), pl.BlockSpec((1,) + wo.shape[1:], grp),
                  pl.BlockSpec((1,) + pw.shape[1:], grp)],
        out_specs=pl.BlockSpec((1, nc, kw), io),
        out_shape=jax.ShapeDtypeStruct(xg.shape, F32),
        scratch_shapes=[pltpu.VMEM((2, nc + 2 * pad, hw), F32), pltpu.VMEM((2, nc + 2 * pad, hw), F32)],
        compiler_params=_params("arbitrary", "arbitrary"),
        name="ssm",
    )(xg, mt, ws, wo, pw)


def _gelu_tanh(x):
    return 0.5 * x * (1.0 + jnp.tanh(math.sqrt(2.0 / math.pi) * (x + 0.044715 * (x * x * x))))


def _mix_kernel(x_ref, a_ref, yg_ref, wglu_ref, bglu_ref, ga_ref, gs_ref, woa_ref, wos_ref, o_ref, ybuf):
    nch = yg_ref.shape[1]
    per = 2 * BLOCKS_PER_VREG
    for gb in range(ybuf.shape[0] // 2):
        grp = lambda i, half: yg_ref[gb * per + i, :, half * LANES:(half + 1) * LANES]
        a = _block_transpose([grp(i, 0) for i in range(BLOCKS_PER_VREG)])
        b = _block_transpose([grp(i, 1) for i in range(BLOCKS_PER_VREG)])
        c = _block_transpose([grp(BLOCKS_PER_VREG + i, 0) for i in range(BLOCKS_PER_VREG)])
        d = _block_transpose([grp(BLOCKS_PER_VREG + i, 1) for i in range(BLOCKS_PER_VREG)])
        for s in range(BLOCKS_PER_VREG):
            ybuf[2 * gb, pl.ds(s, nch, stride=CHUNK), :] = a[s]
            ybuf[2 * gb, pl.ds(BLOCKS_PER_VREG + s, nch, stride=CHUNK), :] = b[s]
            ybuf[2 * gb + 1, pl.ds(s, nch, stride=CHUNK), :] = c[s]
            ybuf[2 * gb + 1, pl.ds(BLOCKS_PER_VREG + s, nch, stride=CHUNK), :] = d[s]
    yg = jnp.concatenate([_gelu_tanh(ybuf[j]) for j in range(ybuf.shape[0])], axis=1)
    gate = jax.nn.sigmoid(jnp.dot(yg.astype(BF16), wglu_ref[...], preferred_element_type=F32) + bglu_ref[...])
    sn = _rms(yg * gate, gs_ref[...]).astype(BF16)
    an = _rms(a_ref[...].astype(F32), ga_ref[...]).astype(BF16)
    o_ref[...] = (x_ref[...] + jnp.dot(an, woa_ref[...], preferred_element_type=F32)
                  + jnp.dot(sn, wos_ref[...], preferred_element_type=F32))


def _mix(x, a, yg, wglu, bglu, ga, gs, woa, wos):
    n, d = x.shape
    tq = _tile(n, MIX_TILES)
    n_g, _, kw = yg.shape
    row = lambda i: (i, 0)
    return pl.pallas_call(
        _mix_kernel,
        grid=(n // tq,),
        in_specs=[pl.BlockSpec((tq, d), row), pl.BlockSpec((tq, a.shape[1]), row),
                  pl.BlockSpec((n_g, tq // CHUNK, kw), lambda i: (0, i, 0)),
                  _resident(wglu), _resident(bglu), _resident(ga), _resident(gs), _resident(woa), _resident(wos)],
        out_specs=pl.BlockSpec((tq, d), row),
        out_shape=jax.ShapeDtypeStruct((n, d), F32),
        scratch_shapes=[pltpu.VMEM((n_g * SSM_GROUP // LANES, tq, LANES), F32)],
        input_output_aliases={0: 0},
        compiler_params=_params("arbitrary"),
        name="mix",
    )(x, a, yg, wglu, bglu, ga, gs, woa, wos)


def _mlp_kernel(x_ref, g_ref, w1_ref, w2_ref, o_ref, h_ref):
    @pl.when(pl.program_id(1) == 0)
    def _():
        x = x_ref[...]
        h_ref[...] = _rms(x, g_ref[...]).astype(BF16)
        o_ref[...] = x

    t = jnp.dot(h_ref[...], w1_ref[...], preferred_element_type=F32)
    t = jnp.square(jnp.maximum(t, 0.0)).astype(BF16)
    o_ref[...] += jnp.dot(t, w2_ref[...], preferred_element_type=F32)


def _mlp(x, g, w1, w2, tf=1024):
    n, d = x.shape
    dff = w1.shape[1]
    tq = _tile(n, DENSE_TILES)
    row = lambda i, j: (i, 0)
    return pl.pallas_call(
        _mlp_kernel,
        grid=(n // tq, dff // tf),
        in_specs=[pl.BlockSpec((tq, d), row), _resident(g),
                  pl.BlockSpec((d, tf), lambda i, j: (0, j)), pl.BlockSpec((tf, d), lambda i, j: (j, 0))],
        out_specs=pl.BlockSpec((tq, d), row),
        out_shape=jax.ShapeDtypeStruct((n, d), F32),
        scratch_shapes=[pltpu.VMEM((tq, d), BF16)],
        input_output_aliases={0: 0},
        compiler_params=_params("arbitrary", "arbitrary"),
        name="mlp",
    )(x, g, w1, w2)


def _final_kernel(x_ref, g_ref, o_ref):
    o_ref[0] = _rms(x_ref[0], g_ref[...])


def _final_norm(x, g, s):
    b, _, d = x.shape
    tq = _tile(s, FINAL_TILES)
    return pl.pallas_call(
        _final_kernel,
        grid=(b, s // tq),
        in_specs=[pl.BlockSpec((pl.Element(1), pl.Element(tq), pl.Element(d)),
                               lambda bi, i: (bi, pl.multiple_of(N_META + i * tq, N_META), 0)), _resident(g)],
        out_specs=pl.BlockSpec((1, tq, d), lambda bi, i: (bi, i, 0)),
        out_shape=jax.ShapeDtypeStruct((b, s, d), F32),
        compiler_params=_params("arbitrary", "arbitrary"),
        name="final_norm",
    )(x, g)


def _rope_table(b, lt, lp):
    half = ROT_DIM // 2
    pos = jnp.arange(lp, dtype=F32)
    inv = ROPE_THETA ** (-jnp.arange(half, dtype=F32) * 2.0 / ROT_DIM)
    ang = pos[:, None] * inv[None, :]
    cos, sin = jnp.cos(ang), jnp.sin(ang)
    rest = HEAD_DIM - ROT_DIM
    c = jnp.concatenate([cos, cos, jnp.ones((lp, rest), F32)], axis=1)
    s1 = jnp.concatenate([-sin, jnp.zeros((lp, HEAD_DIM - half), F32)], axis=1)
    s2 = jnp.concatenate([jnp.zeros((lp, half), F32), sin, jnp.zeros((lp, rest), F32)], axis=1)
    valid = (jnp.arange(lp) < lt).astype(F32)[:, None]
    rep = LANES // HEAD_DIM
    tab = jnp.concatenate([jnp.tile(v * valid, (1, rep)) for v in (c, s1, s2)]
                          + [jnp.broadcast_to(valid, (lp, LANES))], axis=1)
    return jnp.tile(tab, (b, 1))


def _trunk(x_in, layers, meta_tokens, final_g):
    b, s, d = x_in.shape
    lt = N_META + s
    lp = _round_up(lt, ATTN_TILE)
    nc = lp // CHUNK
    meta = jnp.broadcast_to(meta_tokens[None].astype(F32), (b, N_META, d))
    x = jnp.concatenate([meta, x_in, jnp.zeros((b, lp - lt, d), F32)], axis=1).reshape(b * lp, d)
    tab = _rope_table(b, lt, lp)
    for p in layers:
        attn_w, kv_w = p["attn_w"], p["kv_w"]
        q, k, vt, xg = _inproj(x, p["g1"], p["w_in"], p["w_vt"], tab, attn_w, kv_w)
        a = _attention(q.reshape(b, lp, attn_w), k.reshape(b, lp, kv_w), vt, p["sink"], lt)
        yg = _ssm(xg, *p["ssm"], nc)
        x = _mix(x, a.reshape(b * lp, attn_w), yg, p["w_glu"], p["b_glu"], p["g_attn"], p["g_ssm"],
                 p["w_out_a"], p["w_out_s"])
        x = _mlp(x, p["g2"], p["w1"], p["w2"])
    return _final_norm(x.reshape(b, lp, d), final_g, s)


def kernel(x_prompt, x_sample, meta_tokens, norm1_g, w_in, attn_sink, ssm_a_re, ssm_a_im, ssm_log_dt, ssm_b_re, ssm_b_im, ssm_c_re, ssm_c_im, ssm_d, w_glu, b_glu, norm_attn_out, norm_ssm_out, w_out, norm2_g, w_mlp_in, w_mlp_out, final_norm_g):
    depth, d, in_w = w_in.shape
    attn_w = norm_attn_out.shape[1]
    ssm_w = norm_ssm_out.shape[1]
    kv_w = (in_w - attn_w - ssm_w) // 2
    nc_max = max(_round_up(N_META + xs.shape[1], ATTN_TILE) // CHUNK for xs in (x_prompt, x_sample))
    ssm = _ssm_prep(ssm_a_re, ssm_a_im, ssm_log_dt, ssm_b_re, ssm_b_im, ssm_c_re, ssm_c_im, ssm_d,
                    max(1, math.ceil(math.log2(nc_max))))
    row = lambda v: v.astype(F32)[None]
    layers = []
    for l in range(depth):
        layers.append(dict(
            attn_w=attn_w, kv_w=kv_w, ssm=tuple(v[l] for v in ssm),
            g1=row(norm1_g[l]), w_in=w_in[l].astype(BF16), sink=attn_sink[l].astype(F32),
            w_vt=w_in[l, :, attn_w + kv_w:attn_w + 2 * kv_w].T.astype(BF16),
            w_glu=w_glu[l].astype(BF16), b_glu=row(b_glu[l]),
            g_attn=row(norm_attn_out[l]), g_ssm=row(norm_ssm_out[l]),
            w_out_a=w_out[l, :attn_w].astype(BF16), w_out_s=w_out[l, attn_w:].astype(BF16),
            g2=row(norm2_g[l]), w1=w_mlp_in[l].astype(BF16), w2=w_mlp_out[l].astype(BF16)))
    fg = row(final_norm_g)
    return (_trunk(x_prompt, layers, meta_tokens, fg), _trunk(x_sample, layers, meta_tokens, fg))
```

```python
import functools
import math

import jax
import jax.numpy as jnp
from jax import lax
from jax.experimental import pallas as pl
from jax.experimental.pallas import tpu as pltpu

F32 = jnp.float32
BF16 = jnp.bfloat16

LANES = 128
BF16_ROWS = 16
N_META = 16
HEAD_DIM = 64
KV_REP = 4
ROT_DIM = HEAD_DIM // 4
ROPE_THETA = 500000.0
WINDOW = 128
ATTN_TILE = 128
SSM_GROUP = 16
CHUNK = 16
BLOCKS_PER_VREG = LANES // SSM_GROUP
NORM_EPS = 1e-6
NEG_INF = -1e30
LOG2E = math.log2(math.e)
VMEM_LIMIT = 56 * 1024 * 1024
DENSE_TILES = (768, 384, 256, 128)
MIX_TILES = (384, 256, 128)
FINAL_TILES = (1024, 512, 256, 128)
N_CHUNK_COLS = 512
SCAN_ROWS = 128


def _round_up(n, m):
    return (n + m - 1) // m * m


def _tile(n, candidates):
    for t in candidates:
        if n % t == 0:
            return t
    raise ValueError(f"{n} rows cannot be tiled by any of {candidates}")


def _rms(x, g):
    return x * lax.rsqrt(jnp.mean(x * x, axis=-1, keepdims=True) + NORM_EPS) * g


def _params(*sem):
    return pltpu.CompilerParams(dimension_semantics=sem, vmem_limit_bytes=VMEM_LIMIT)


def _resident(arr):
    nd = arr.ndim
    return pl.BlockSpec(arr.shape, lambda *_: (0,) * nd, pipeline_mode=pl.Buffered(1))


def _block_transpose(parts):
    blk = lax.broadcasted_iota(jnp.int32, parts[0].shape, 1) // SSM_GROUP
    b = BLOCKS_PER_VREG // 2
    while b:
        upper = (blk & b) != 0
        nxt = list(parts)
        for i in range(len(parts)):
            if i & b == 0:
                lo, hi = parts[i], parts[i + b]
                nxt[i] = jnp.where(upper, pltpu.roll(hi, SSM_GROUP * b, 1), lo)
                nxt[i + b] = jnp.where(upper, hi, pltpu.roll(lo, LANES - SSM_GROUP * b, 1))
        parts = nxt
        b //= 2
    return parts


def _inproj_kernel(x_ref, g_ref, w_ref, wvt_ref, tab_ref, q_ref, k_ref, vt_ref, xg_ref, ubuf, *, attn_w, kv_w):
    tq, d = x_ref.shape
    cos, s1, s2 = tab_ref[:, 0:LANES], tab_ref[:, LANES:2 * LANES], tab_ref[:, 2 * LANES:3 * LANES]
    valid = tab_ref[:, 3 * LANES:4 * LANES] > 0.0
    h = jnp.where(jnp.concatenate([valid] * (d // LANES), axis=1), _rms(x_ref[...], g_ref[...]), 0.0).astype(BF16)
    half = ROT_DIM // 2
    scale = HEAD_DIM ** -0.5 * LOG2E

    def rope(blk):
        return blk * cos + pltpu.roll(blk, LANES - half, 1) * s1 + pltpu.roll(blk, half, 1) * s2

    u0 = attn_w + 2 * kv_w
    spans = [(c, min(N_CHUNK_COLS, w_ref.shape[1] - c)) for c in range(u0, w_ref.shape[1], N_CHUNK_COLS)]
    spans += [(c, min(N_CHUNK_COLS, attn_w + kv_w - c)) for c in range(0, attn_w + kv_w, N_CHUNK_COLS)]
    for c0, width in spans:
        z = jnp.dot(h, w_ref[:, c0:c0 + width], preferred_element_type=F32)
        for j in range(width // LANES):
            col = c0 + j * LANES
            blk = z[:, j * LANES:(j + 1) * LANES]
            if col < attn_w:
                q_ref[:, col:col + LANES] = (rope(blk) * scale).astype(BF16)
            elif col < attn_w + kv_w:
                k_ref[:, col - attn_w:col - attn_w + LANES] = rope(blk).astype(BF16)
            else:
                ubuf[(col - u0) // LANES] = blk
    vt_ref[...] = lax.dot_general(wvt_ref[...], h, (((1,), (1,)), ((), ())),
                                  preferred_element_type=F32).astype(BF16)

    nch = tq // CHUNK
    per = 2 * BLOCKS_PER_VREG
    for gb in range(ubuf.shape[0] // 2):
        lo = [ubuf[2 * gb, pl.ds(s, nch, stride=CHUNK), :] for s in range(CHUNK)]
        hi = [ubuf[2 * gb + 1, pl.ds(s, nch, stride=CHUNK), :] for s in range(CHUNK)]
        a, b = _block_transpose(lo[:BLOCKS_PER_VREG]), _block_transpose(lo[BLOCKS_PER_VREG:])
        c, d = _block_transpose(hi[:BLOCKS_PER_VREG]), _block_transpose(hi[BLOCKS_PER_VREG:])
        for i in range(BLOCKS_PER_VREG):
            xg_ref[gb * per + i, :, 0:LANES] = a[i]
            xg_ref[gb * per + i, :, LANES:2 * LANES] = b[i]
            xg_ref[gb * per + BLOCKS_PER_VREG + i, :, 0:LANES] = c[i]
            xg_ref[gb * per + BLOCKS_PER_VREG + i, :, LANES:2 * LANES] = d[i]


def _inproj(x, g, w, wvt, tab, attn_w, kv_w):
    n, d = x.shape
    tq = _tile(n, DENSE_TILES)
    ssm_w = w.shape[1] - attn_w - 2 * kv_w
    n_g = ssm_w // SSM_GROUP
    row = lambda i: (i, 0)
    return pl.pallas_call(
        functools.partial(_inproj_kernel, attn_w=attn_w, kv_w=kv_w),
        grid=(n // tq,),
        in_specs=[pl.BlockSpec((tq, d), row), _resident(g), _resident(w), _resident(wvt),
                  pl.BlockSpec((tq, 4 * LANES), row)],
        out_specs=[pl.BlockSpec((tq, attn_w), row), pl.BlockSpec((tq, kv_w), row),
                   pl.BlockSpec((kv_w, tq), lambda i: (0, i)),
                   pl.BlockSpec((n_g, tq // CHUNK, CHUNK * SSM_GROUP), lambda i: (0, i, 0))],
        out_shape=[jax.ShapeDtypeStruct((n, attn_w), BF16), jax.ShapeDtypeStruct((n, kv_w), BF16),
                   jax.ShapeDtypeStruct((kv_w, n), BF16),
                   jax.ShapeDtypeStruct((n_g, n // CHUNK, CHUNK * SSM_GROUP), F32)],
        scratch_shapes=[pltpu.VMEM((ssm_w // LANES, tq, LANES), F32)],
        compiler_params=_params("arbitrary"),
        name="inproj",
    )(x, g, w, wvt, tab)


def _attn_kernel(sink_ref, q_ref, kp_ref, kc_ref, kn_ref, k0_ref, vp_ref, vc_ref, vn_ref, v0_ref, o_ref, *,
                 lt, n_heads):
    a = pl.program_id(1)
    t = ATTN_TILE
    q = q_ref[0]
    kall = jnp.concatenate([kp_ref[0], kc_ref[0], kn_ref[0], k0_ref[0]], axis=0)
    vtall = jnp.concatenate([vp_ref[...], vc_ref[...], vn_ref[...], v0_ref[...]], axis=1)
    qpos = a * t + lax.broadcasted_iota(jnp.int32, (3 * t, t), 1)
    kpos = (a - 1) * t + lax.broadcasted_iota(jnp.int32, (3 * t, t), 0)
    band = (jnp.abs(qpos - kpos) <= WINDOW) & (kpos >= 0) & (kpos < lt)
    mpos = lax.broadcasted_iota(jnp.int32, (t, t), 0)
    far_meta = (mpos < N_META) & (a * t + lax.broadcasted_iota(jnp.int32, (t, t), 1) - mpos > WINDOW)
    bias = jnp.where(jnp.concatenate([band, far_meta], axis=0), 0.0, NEG_INF)
    bias = jnp.concatenate([bias] * n_heads, axis=1)
    nt = (((1,), (1,)), ((), ()))
    n_groups = n_heads // KV_REP
    zeros = jnp.zeros((t, HEAD_DIM), BF16)

    def padded(h):
        g = h // KV_REP
        return jnp.concatenate([zeros] * g + [q[:, h * HEAD_DIM:(h + 1) * HEAD_DIM]] + [zeros] * (n_groups - 1 - g),
                               axis=1)

    qq = jnp.concatenate([padded(h) for h in range(n_heads)], axis=0)
    s = lax.dot_general(kall, qq, nt, preferred_element_type=F32) + bias
    sink = jnp.concatenate([jnp.full((1, t), sink_ref[h] * LOG2E, F32) for h in range(n_heads)], axis=1)
    m = jnp.maximum(jnp.max(s, axis=0, keepdims=True), sink)
    e = jnp.exp2(s - m).astype(BF16)
    sink_e = jnp.exp2(sink - m)
    ones = jnp.ones((BF16_ROWS, 4 * t), BF16)
    gw = KV_REP * t
    for g in range(n_groups):
        ks = slice(g * HEAD_DIM, (g + 1) * HEAD_DIM)
        oa = jnp.dot(jnp.concatenate([vtall[ks, :], ones], axis=0), e[:, g * gw:(g + 1) * gw],
                     preferred_element_type=F32)
        den = oa[HEAD_DIM:HEAD_DIM + 1] + sink_e[:, g * gw:(g + 1) * gw]
        o = oa[:HEAD_DIM] * (1.0 / den)
        for r in range(0, KV_REP, 2):
            pair = jnp.concatenate([o[:, r * t:(r + 1) * t], o[:, (r + 1) * t:(r + 2) * t]], axis=0)
            lo = (g * KV_REP + r) * HEAD_DIM
            o_ref[0, :, lo:lo + 2 * HEAD_DIM] = pair.T.astype(BF16)


def _attention(q, k, vt, sink, lt):
    b, lp, attn_w = q.shape
    kv_w = k.shape[2]
    t = ATTN_TILE
    nt = lp // t
    cur = lambda bi, a: (bi, a, 0)
    prev = lambda bi, a: (bi, jnp.maximum(a - 1, 0), 0)
    nxt = lambda bi, a: (bi, jnp.minimum(a + 1, nt - 1), 0)
    first = lambda bi, a: (bi, 0, 0)
    kspec = lambda m: pl.BlockSpec((1, t, kv_w), m)
    vspec = lambda m: pl.BlockSpec((kv_w, t), lambda bi, a: (0, bi * nt + m(bi, a)[1]))
    return pl.pallas_call(
        functools.partial(_attn_kernel, lt=lt, n_heads=attn_w // HEAD_DIM),
        grid=(b, nt),
        in_specs=[pl.BlockSpec(memory_space=pltpu.SMEM), pl.BlockSpec((1, t, attn_w), cur),
                  kspec(prev), kspec(cur), kspec(nxt), kspec(first),
                  vspec(prev), vspec(cur), vspec(nxt), vspec(first)],
        out_specs=pl.BlockSpec((1, t, attn_w), cur),
        out_shape=jax.ShapeDtypeStruct((b, lp, attn_w), BF16),
        compiler_params=_params("arbitrary", "arbitrary"),
        name="attention",
    )(sink, q, k, k, k, k, vt, vt, vt, vt)


def _ssm_prep(a_re, a_im, log_dt, b_re, b_im, c_re, c_im, d, nsteps):
    hp = lax.Precision.HIGHEST
    t = CHUNK
    a_re, a_im, b_re, b_im = (v.astype(F32) for v in (a_re, a_im, b_re, b_im))
    n_l, _, n_g, n_p, n_h = b_re.shape
    dt = jnp.exp(log_dt.astype(F32))[..., None]

    def cpow(j):
        j = j.astype(F32)
        mag = jnp.exp((a_re * dt)[..., None] * j)
        ang = (a_im * dt)[..., None] * j
        return mag * jnp.cos(ang), mag * jnp.sin(ang)

    mag = jnp.exp(a_re * dt)
    ang = a_im * dt
    abr, abi = mag * jnp.cos(ang), mag * jnp.sin(ang)
    den = a_re * a_re + a_im * a_im
    nr = abr - 1.0
    cr = (nr * a_re + abi * a_im) / den
    ci = (abi * a_re - nr * a_im) / den
    bbr = cr[..., None] * b_re - ci[..., None] * b_im
    bbi = cr[..., None] * b_im + ci[..., None] * b_re

    pr0, pi0 = cpow(jnp.arange(t + 1))
    pr, pi = (jnp.repeat(v, n_h, axis=-1) for v in (pr0, pi0))
    c_ret = jnp.tile(jnp.swapaxes(c_re.astype(F32), -1, -2), (1, 1, 1, 1, t + 1))
    c_imt = jnp.tile(jnp.swapaxes(c_im.astype(F32), -1, -2), (1, 1, 1, 1, t + 1))
    car = c_ret * pr - c_imt * pi
    cai = c_ret * pi + c_imt * pr
    w = t * n_h
    kj = (jnp.einsum('ldgpx,ldgpk->ldgkx', car[..., :w], bbr, precision=hp)
          - jnp.einsum('ldgpx,ldgpk->ldgkx', cai[..., :w], bbi, precision=hp))
    kf, kb = kj[:, 0], kj[:, 1]
    dmat = jnp.eye(n_h, dtype=F32) * d.astype(F32).reshape(n_l, n_g, 1, n_h)
    centre = kf[..., :n_h] + kb[..., :n_h] + dmat
    kb_rev = jnp.flip(kb.reshape(n_l, n_g, n_h, t, n_h), axis=3).reshape(n_l, n_g, n_h, w)
    kall = jnp.concatenate([kb_rev[..., :w - n_h], centre, kf[..., n_h:]], axis=-1)
    mt = jnp.stack([kall[..., (t - 1 - s) * n_h:(2 * t - 1 - s) * n_h] for s in range(t)], axis=2)
    mt = mt.reshape(n_l, n_g, w, w)

    def b_pow(pw_r, pw_i, d_idx):
        pw_r, pw_i = (jnp.swapaxes(v, -1, -2)[..., None, :] for v in (pw_r, pw_i))
        br, bi = (jnp.swapaxes(v[:, d_idx], -1, -2)[:, :, None] for v in (bbr, bbi))
        shape = (n_l, n_g, t * n_h, n_p)
        return (pw_r * br - pw_i * bi).reshape(shape), (pw_r * bi + pw_i * br).reshape(shape)

    wf_r, wf_i = b_pow(pr0[:, 0, ..., t - 1::-1], pi0[:, 0, ..., t - 1::-1], 0)
    wb_r, wb_i = b_pow(pr0[:, 1, ..., :t], pi0[:, 1, ..., :t], 1)
    ws = jnp.concatenate([wf_r, wb_r, wf_i, wb_i], axis=-1)

    fwd_rows = lambda v: v[:, 0, ..., n_h:]
    bwd_rows = lambda v: jnp.flip(v[:, 1].reshape(n_l, n_g, n_p, t + 1, n_h), axis=3)[..., :t, :].reshape(
        n_l, n_g, n_p, w)
    wo = jnp.concatenate([fwd_rows(car), bwd_rows(car), -fwd_rows(cai), -bwd_rows(cai)], axis=2)

    sr, si = cpow(t * (2 ** jnp.arange(nsteps)))
    lanes = lambda v: jnp.concatenate([v[:, 0], v[:, 1]], axis=2)
    pw = jnp.stack([lanes(sr), lanes(si)], axis=-1)
    pw = jnp.transpose(pw, (0, 1, 3, 4, 2)).reshape(n_l, n_g, 2 * nsteps, 2 * n_p)
    return mt.astype(BF16), ws.astype(BF16), wo.astype(BF16), pw


def _ssm_kernel(x_ref, mt_ref, ws_ref, wo_ref, pw_ref, y_ref, rbuf, ibuf, *, nc, nsteps, pad):
    x = x_ref[0].astype(BF16)
    s = jnp.dot(x, ws_ref[0], preferred_element_type=F32)
    hw = s.shape[1] // 2
    zeros = jnp.zeros((pad, hw), F32)
    for slot in range(2):
        for buf, part in ((rbuf, s[:, :hw]), (ibuf, s[:, hw:])):
            buf[slot, 0:pad, :] = zeros
            buf[slot, pad + nc:pad + nc + pad, :] = zeros
            if slot == 0:
                buf[slot, pad:pad + nc, :] = part
    rows = [(r0, min(SCAN_ROWS, nc - r0)) for r0 in range(0, nc, SCAN_ROWS)]

    def shifted(buf, slot, d, r0, n):
        fwd = lax.broadcasted_iota(jnp.int32, (n, hw), 1) < hw // 2
        lo = pad + r0
        return jnp.where(fwd, buf[slot, lo - d:lo - d + n, :], buf[slot, lo + d:lo + d + n, :])

    d = 1
    for k in range(nsteps):
        src, dst = k % 2, (k + 1) % 2
        ar, ai = pw_ref[0, 2 * k:2 * k + 1, :], pw_ref[0, 2 * k + 1:2 * k + 2, :]
        for r0, n in rows:
            lo = pad + r0
            sr, si = shifted(rbuf, src, d, r0, n), shifted(ibuf, src, d, r0, n)
            rbuf[dst, lo:lo + n, :] = rbuf[src, lo:lo + n, :] + ar * sr - ai * si
            ibuf[dst, lo:lo + n, :] = ibuf[src, lo:lo + n, :] + ar * si + ai * sr
        d *= 2
    last = nsteps % 2
    hin = jnp.concatenate([shifted(rbuf, last, 1, 0, nc), shifted(ibuf, last, 1, 0, nc)],
                          axis=1).astype(BF16)
    y_ref[0] = (jnp.dot(x, mt_ref[0], preferred_element_type=F32)
                + jnp.dot(hin, wo_ref[0], preferred_element_type=F32))


def _ssm(xg, mt, ws, wo, pw, nc):
    n_g, rows, kw = xg.shape
    nsteps = max(1, math.ceil(math.log2(nc)))
    pad = max(8, 2 ** (nsteps - 1))
    grp = lambda bi, g: (g, 0, 0)
    io = lambda bi, g: (g, bi, 0)
    hw = ws.shape[2] // 2
    return pl.pallas_call(
        functools.partial(_ssm_kernel, nc=nc, nsteps=nsteps, pad=pad),
        grid=(rows // nc, n_g),
        in_specs=[pl.BlockSpec((1, nc, kw), io), pl.BlockSpec((1,) + mt.shape[1:], grp),
                  pl.BlockSpec((1,) + ws.shape[1:], grp), pl.BlockSpec((1,) + wo.shape[1:], grp),
                  pl.BlockSpec((1,) + pw.shape[1:], grp)],
        out_specs=pl.BlockSpec((1, nc, kw), io),
        out_shape=jax.ShapeDtypeStruct(xg.shape, F32),
        scratch_shapes=[pltpu.VMEM((2, nc + 2 * pad, hw), F32), pltpu.VMEM((2, nc + 2 * pad, hw), F32)],
        compiler_params=_params("arbitrary", "arbitrary"),
        name="ssm",
    )(xg, mt, ws, wo, pw)


def _gelu_tanh(x):
    return 0.5 * x * (1.0 + jnp.tanh(math.sqrt(2.0 / math.pi) * (x + 0.044715 * (x * x * x))))


def _mix_kernel(x_ref, a_ref, yg_ref, wglu_ref, bglu_ref, ga_ref, gs_ref, woa_ref, wos_ref, o_ref, ybuf):
    an = _rms(a_ref[...].astype(F32), ga_ref[...]).astype(BF16)
    o_ref[...] = x_ref[...] + jnp.dot(an, woa_ref[...], preferred_element_type=F32)
    nch = yg_ref.shape[1]
    per = 2 * BLOCKS_PER_VREG
    for gb in range(ybuf.shape[0] // 2):
        grp = lambda i, half: yg_ref[gb * per + i, :, half * LANES:(half + 1) * LANES]
        a = _block_transpose([grp(i, 0) for i in range(BLOCKS_PER_VREG)])
        b = _block_transpose([grp(i, 1) for i in range(BLOCKS_PER_VREG)])
        c = _block_transpose([grp(BLOCKS_PER_VREG + i, 0) for i in range(BLOCKS_PER_VREG)])
        d = _block_transpose([grp(BLOCKS_PER_VREG + i, 1) for i in range(BLOCKS_PER_VREG)])
        for s in range(BLOCKS_PER_VREG):
            ybuf[2 * gb, pl.ds(s, nch, stride=CHUNK), :] = a[s]
            ybuf[2 * gb, pl.ds(BLOCKS_PER_VREG + s, nch, stride=CHUNK), :] = b[s]
            ybuf[2 * gb + 1, pl.ds(s, nch, stride=CHUNK), :] = c[s]
            ybuf[2 * gb + 1, pl.ds(BLOCKS_PER_VREG + s, nch, stride=CHUNK), :] = d[s]
    yg = jnp.concatenate([_gelu_tanh(ybuf[j]) for j in range(ybuf.shape[0])], axis=1)
    gate = jax.nn.sigmoid(jnp.dot(yg.astype(BF16), wglu_ref[...], preferred_element_type=F32) + bglu_ref[...])
    sn = _rms(yg * gate, gs_ref[...]).astype(BF16)
    o_ref[...] += jnp.dot(sn, wos_ref[...], preferred_element_type=F32)


def _mix(x, a, yg, wglu, bglu, ga, gs, woa, wos):
    n, d = x.shape
    tq = _tile(n, MIX_TILES)
    n_g, _, kw = yg.shape
    row = lambda i: (i, 0)
    return pl.pallas_call(
        _mix_kernel,
        grid=(n // tq,),
        in_specs=[pl.BlockSpec((tq, d), row), pl.BlockSpec((tq, a.shape[1]), row),
                  pl.BlockSpec((n_g, tq // CHUNK, kw), lambda i: (0, i, 0)),
                  _resident(wglu), _resident(bglu), _resident(ga), _resident(gs), _resident(woa), _resident(wos)],
        out_specs=pl.BlockSpec((tq, d), row),
        out_shape=jax.ShapeDtypeStruct((n, d), F32),
        scratch_shapes=[pltpu.VMEM((n_g * SSM_GROUP // LANES, tq, LANES), F32)],
        input_output_aliases={0: 0},
        compiler_params=_params("arbitrary"),
        name="mix",
    )(x, a, yg, wglu, bglu, ga, gs, woa, wos)


def _mlp_kernel(x_ref, g_ref, w1_ref, w2_ref, o_ref, h_ref):
    @pl.when(pl.program_id(1) == 0)
    def _():
        x = x_ref[...]
        h_ref[...] = _rms(x, g_ref[...]).astype(BF16)
        o_ref[...] = x

    t = jnp.dot(h_ref[...], w1_ref[...], preferred_element_type=F32)
    t = jnp.square(jnp.maximum(t, 0.0)).astype(BF16)
    o_ref[...] += jnp.dot(t, w2_ref[...], preferred_element_type=F32)


def _mlp(x, g, w1, w2, tf=1024):
    n, d = x.shape
    dff = w1.shape[1]
    tq = _tile(n, DENSE_TILES)
    row = lambda i, j: (i, 0)
    return pl.pallas_call(
        _mlp_kernel,
        grid=(n // tq, dff // tf),
        in_specs=[pl.BlockSpec((tq, d), row), _resident(g),
                  pl.BlockSpec((d, tf), lambda i, j: (0, j)), pl.BlockSpec((tf, d), lambda i, j: (j, 0))],
        out_specs=pl.BlockSpec((tq, d), row),
        out_shape=jax.ShapeDtypeStruct((n, d), F32),
        scratch_shapes=[pltpu.VMEM((tq, d), BF16)],
        input_output_aliases={0: 0},
        compiler_params=_params("arbitrary", "arbitrary"),
        name="mlp",
    )(x, g, w1, w2)


def _final_kernel(x_ref, g_ref, o_ref):
    o_ref[0] = _rms(x_ref[0], g_ref[...])


def _final_norm(x, g, s):
    b, _, d = x.shape
    tq = _tile(s, FINAL_TILES)
    return pl.pallas_call(
        _final_kernel,
        grid=(b, s // tq),
        in_specs=[pl.BlockSpec((pl.Element(1), pl.Element(tq), pl.Element(d)),
                               lambda bi, i: (bi, pl.multiple_of(N_META + i * tq, N_META), 0)), _resident(g)],
        out_specs=pl.BlockSpec((1, tq, d), lambda bi, i: (bi, i, 0)),
        out_shape=jax.ShapeDtypeStruct((b, s, d), F32),
        compiler_params=_params("arbitrary", "arbitrary"),
        name="final_norm",
    )(x, g)


def _rope_table(b, lt, lp):
    half = ROT_DIM // 2
    pos = jnp.arange(lp, dtype=F32)
    inv = ROPE_THETA ** (-jnp.arange(half, dtype=F32) * 2.0 / ROT_DIM)
    ang = pos[:, None] * inv[None, :]
    cos, sin = jnp.cos(ang), jnp.sin(ang)
    rest = HEAD_DIM - ROT_DIM
    c = jnp.concatenate([cos, cos, jnp.ones((lp, rest), F32)], axis=1)
    s1 = jnp.concatenate([-sin, jnp.zeros((lp, HEAD_DIM - half), F32)], axis=1)
    s2 = jnp.concatenate([jnp.zeros((lp, half), F32), sin, jnp.zeros((lp, rest), F32)], axis=1)
    valid = (jnp.arange(lp) < lt).astype(F32)[:, None]
    rep = LANES // HEAD_DIM
    tab = jnp.concatenate([jnp.tile(v * valid, (1, rep)) for v in (c, s1, s2)]
                          + [jnp.broadcast_to(valid, (lp, LANES))], axis=1)
    return jnp.tile(tab, (b, 1))


def _trunk(x_in, layers, meta_tokens, final_g):
    b, s, d = x_in.shape
    lt = N_META + s
    lp = _round_up(lt, ATTN_TILE)
    nc = lp // CHUNK
    meta = jnp.broadcast_to(meta_tokens[None].astype(F32), (b, N_META, d))
    x = jnp.concatenate([meta, x_in, jnp.zeros((b, lp - lt, d), F32)], axis=1).reshape(b * lp, d)
    tab = _rope_table(b, lt, lp)
    for p in layers:
        attn_w, kv_w = p["attn_w"], p["kv_w"]
        q, k, vt, xg = _inproj(x, p["g1"], p["w_in"], p["w_vt"], tab, attn_w, kv_w)
        a = _attention(q.reshape(b, lp, attn_w), k.reshape(b, lp, kv_w), vt, p["sink"], lt)
        yg = _ssm(xg, *p["ssm"], nc)
        x = _mix(x, a.reshape(b * lp, attn_w), yg, p["w_glu"], p["b_glu"], p["g_attn"], p["g_ssm"],
                 p["w_out_a"], p["w_out_s"])
        x = _mlp(x, p["g2"], p["w1"], p["w2"])
    return _final_norm(x.reshape(b, lp, d), final_g, s)


def kernel(x_prompt, x_sample, meta_tokens, norm1_g, w_in, attn_sink, ssm_a_re, ssm_a_im, ssm_log_dt, ssm_b_re, ssm_b_im, ssm_c_re, ssm_c_im, ssm_d, w_glu, b_glu, norm_attn_out, norm_ssm_out, w_out, norm2_g, w_mlp_in, w_mlp_out, final_norm_g):
    depth, d, in_w = w_in.shape
    attn_w = norm_attn_out.shape[1]
    ssm_w = norm_ssm_out.shape[1]
    kv_w = (in_w - attn_w - ssm_w) // 2
    nc_max = max(_round_up(N_META + xs.shape[1], ATTN_TILE) // CHUNK for xs in (x_prompt, x_sample))
    ssm = _ssm_prep(ssm_a_re, ssm_a_im, ssm_log_dt, ssm_b_re, ssm_b_im, ssm_c_re, ssm_c_im, ssm_d,
                    max(1, math.ceil(math.log2(nc_max))))
    row = lambda v: v.astype(F32)[None]
    layers = []
    for l in range(depth):
        layers.append(dict(
            attn_w=attn_w, kv_w=kv_w, ssm=tuple(v[l] for v in ssm),
            g1=row(norm1_g[l]), w_in=w_in[l].astype(BF16), sink=attn_sink[l].astype(F32),
            w_vt=w_in[l, :, attn_w + kv_w:attn_w + 2 * kv_w].T.astype(BF16),
            w_glu=w_glu[l].astype(BF16), b_glu=row(b_glu[l]),
            g_attn=row(norm_attn_out[l]), g_ssm=row(norm_ssm_out[l]),
            w_out_a=w_out[l, :attn_w].astype(BF16), w_out_s=w_out[l, attn_w:].astype(BF16),
            g2=row(norm2_g[l]), w1=w_mlp_in[l].astype(BF16), w2=w_mlp_out[l].astype(BF16)))
    fg = row(final_norm_g)
    return (_trunk(x_prompt, layers, meta_tokens, fg), _trunk(x_sample, layers, meta_tokens, fg))
```

```python
import functools
import math

import jax
import jax.numpy as jnp
from jax import lax
from jax.experimental import pallas as pl
from jax.experimental.pallas import tpu as pltpu

F32 = jnp.float32
BF16 = jnp.bfloat16

LANES = 128
BF16_ROWS = 16
N_META = 16
HEAD_DIM = 64
KV_REP = 4
ROT_DIM = HEAD_DIM // 4
ROPE_THETA = 500000.0
WINDOW = 128
ATTN_TILE = 128
SSM_GROUP = 16
CHUNK = 16
BLOCKS_PER_VREG = LANES // SSM_GROUP
NORM_EPS = 1e-6
NEG_INF = -1e30
LOG2E = math.log2(math.e)
VMEM_LIMIT = 56 * 1024 * 1024
DENSE_TILES = (768, 384, 256, 128)
MIX_TILES = (384, 256, 128)
FINAL_TILES = (1024, 512, 256, 128)
N_CHUNK_COLS = 512
SCAN_ROWS = 128


def _round_up(n, m):
    return (n + m - 1) // m * m


def _tile(n, candidates):
    for t in candidates:
        if n % t == 0:
            return t
    raise ValueError(f"{n} rows cannot be tiled by any of {candidates}")


def _rms(x, g):
    return x * lax.rsqrt(jnp.mean(x * x, axis=-1, keepdims=True) + NORM_EPS) * g


def _params(*sem):
    return pltpu.CompilerParams(dimension_semantics=sem, vmem_limit_bytes=VMEM_LIMIT)


def _resident(arr):
    nd = arr.ndim
    return pl.BlockSpec(arr.shape, lambda *_: (0,) * nd, pipeline_mode=pl.Buffered(1))


def _block_transpose(parts):
    blk = lax.broadcasted_iota(jnp.int32, parts[0].shape, 1) // SSM_GROUP
    b = BLOCKS_PER_VREG // 2
    while b:
        upper = (blk & b) != 0
        nxt = list(parts)
        for i in range(len(parts)):
            if i & b == 0:
                lo, hi = parts[i], parts[i + b]
                nxt[i] = jnp.where(upper, pltpu.roll(hi, SSM_GROUP * b, 1), lo)
                nxt[i + b] = jnp.where(upper, hi, pltpu.roll(lo, LANES - SSM_GROUP * b, 1))
        parts = nxt
        b //= 2
    return parts


def _inproj_kernel(x_ref, g_ref, w_ref, wvt_ref, tab_ref, q_ref, k_ref, vt_ref, xg_ref, ubuf, *, attn_w, kv_w):
    tq, d = x_ref.shape
    cos, s1, s2 = tab_ref[:, 0:LANES], tab_ref[:, LANES:2 * LANES], tab_ref[:, 2 * LANES:3 * LANES]
    valid = tab_ref[:, 3 * LANES:4 * LANES] > 0.0
    h = jnp.where(jnp.concatenate([valid] * (d // LANES), axis=1), _rms(x_ref[...], g_ref[...]), 0.0).astype(BF16)
    half = ROT_DIM // 2
    scale = HEAD_DIM ** -0.5 * LOG2E

    def rope(blk):
        return blk * cos + pltpu.roll(blk, LANES - half, 1) * s1 + pltpu.roll(blk, half, 1) * s2

    u0 = attn_w + 2 * kv_w
    spans = [(c, min(N_CHUNK_COLS, w_ref.shape[1] - c)) for c in range(u0, w_ref.shape[1], N_CHUNK_COLS)]
    spans += [(c, min(N_CHUNK_COLS, attn_w + kv_w - c)) for c in range(0, attn_w + kv_w, N_CHUNK_COLS)]
    for c0, width in spans:
        z = jnp.dot(h, w_ref[:, c0:c0 + width], preferred_element_type=F32)
        for j in range(width // LANES):
            col = c0 + j * LANES
            blk = z[:, j * LANES:(j + 1) * LANES]
            if col < attn_w:
                q_ref[:, col:col + LANES] = (rope(blk) * scale).astype(BF16)
            elif col < attn_w + kv_w:
                k_ref[:, col - attn_w:col - attn_w + LANES] = rope(blk).astype(BF16)
            else:
                ubuf[(col - u0) // LANES] = blk
    vt_ref[...] = lax.dot_general(wvt_ref[...], h, (((1,), (1,)), ((), ())),
                                  preferred_element_type=F32).astype(BF16)

    nch = tq // CHUNK
    per = 2 * BLOCKS_PER_VREG
    for gb in range(ubuf.shape[0] // 2):
        lo = [ubuf[2 * gb, pl.ds(s, nch, stride=CHUNK), :] for s in range(CHUNK)]
        hi = [ubuf[2 * gb + 1, pl.ds(s, nch, stride=CHUNK), :] for s in range(CHUNK)]
        a, b = _block_transpose(lo[:BLOCKS_PER_VREG]), _block_transpose(lo[BLOCKS_PER_VREG:])
        c, d = _block_transpose(hi[:BLOCKS_PER_VREG]), _block_transpose(hi[BLOCKS_PER_VREG:])
        for i in range(BLOCKS_PER_VREG):
            xg_ref[gb * per + i, :, 0:LANES] = a[i]
            xg_ref[gb * per + i, :, LANES:2 * LANES] = b[i]
            xg_ref[gb * per + BLOCKS_PER_VREG + i, :, 0:LANES] = c[i]
            xg_ref[gb * per + BLOCKS_PER_VREG + i, :, LANES:2 * LANES] = d[i]


def _inproj(x, g, w, wvt, tab, attn_w, kv_w):
    n, d = x.shape
    tq = _tile(n, DENSE_TILES)
    ssm_w = w.shape[1] - attn_w - 2 * kv_w
    n_g = ssm_w // SSM_GROUP
    row = lambda i: (i, 0)
    return pl.pallas_call(
        functools.partial(_inproj_kernel, attn_w=attn_w, kv_w=kv_w),
        grid=(n // tq,),
        in_specs=[pl.BlockSpec((tq, d), row), _resident(g), _resident(w), _resident(wvt),
                  pl.BlockSpec((tq, 4 * LANES), row)],
        out_specs=[pl.BlockSpec((tq, attn_w), row), pl.BlockSpec((tq, kv_w), row),
                   pl.BlockSpec((kv_w, tq), lambda i: (0, i)),
                   pl.BlockSpec((n_g, tq // CHUNK, CHUNK * SSM_GROUP), lambda i: (0, i, 0))],
        out_shape=[jax.ShapeDtypeStruct((n, attn_w), BF16), jax.ShapeDtypeStruct((n, kv_w), BF16),
                   jax.ShapeDtypeStruct((kv_w, n), BF16),
                   jax.ShapeDtypeStruct((n_g, n // CHUNK, CHUNK * SSM_GROUP), F32)],
        scratch_shapes=[pltpu.VMEM((ssm_w // LANES, tq, LANES), F32)],
        compiler_params=_params("arbitrary"),
        name="inproj",
    )(x, g, w, wvt, tab)


def _attn_kernel(sink_ref, q_ref, kp_ref, kc_ref, kn_ref, k0_ref, vp_ref, vc_ref, vn_ref, v0_ref, o_ref, *,
                 lt, n_heads):
    a = pl.program_id(1)
    t = ATTN_TILE
    q = q_ref[0]
    kall = jnp.concatenate([kp_ref[0], kc_ref[0], kn_ref[0], k0_ref[0]], axis=0)
    vtall = jnp.concatenate([vp_ref[...], vc_ref[...], vn_ref[...], v0_ref[...]], axis=1)
    qpos = a * t + lax.broadcasted_iota(jnp.int32, (3 * t, t), 1)
    kpos = (a - 1) * t + lax.broadcasted_iota(jnp.int32, (3 * t, t), 0)
    band = (jnp.abs(qpos - kpos) <= WINDOW) & (kpos >= 0) & (kpos < lt)
    mpos = lax.broadcasted_iota(jnp.int32, (t, t), 0)
    far_meta = (mpos < N_META) & (a * t + lax.broadcasted_iota(jnp.int32, (t, t), 1) - mpos > WINDOW)
    bias = jnp.where(jnp.concatenate([band, far_meta], axis=0), 0.0, NEG_INF)
    bias = jnp.concatenate([bias] * n_heads, axis=1)
    nt = (((1,), (1,)), ((), ()))
    n_groups = n_heads // KV_REP
    zeros = jnp.zeros((t, HEAD_DIM), BF16)

    def padded(h):
        g = h // KV_REP
        return jnp.concatenate([zeros] * g + [q[:, h * HEAD_DIM:(h + 1) * HEAD_DIM]] + [zeros] * (n_groups - 1 - g),
                               axis=1)

    qq = jnp.concatenate([padded(h) for h in range(n_heads)], axis=0)
    s = lax.dot_general(kall, qq, nt, preferred_element_type=F32) + bias
    sink = jnp.concatenate([jnp.full((1, t), sink_ref[h] * LOG2E, F32) for h in range(n_heads)], axis=1)
    m = jnp.maximum(jnp.max(s, axis=0, keepdims=True), sink)
    e = jnp.exp2(s - m).astype(BF16)
    sink_e = jnp.exp2(sink - m)
    ones = jnp.ones((BF16_ROWS, 4 * t), BF16)
    gw = KV_REP * t
    for g in range(n_groups):
        ks = slice(g * HEAD_DIM, (g + 1) * HEAD_DIM)
        oa = jnp.dot(jnp.concatenate([vtall[ks, :], ones], axis=0), e[:, g * gw:(g + 1) * gw],
                     preferred_element_type=F32)
        den = oa[HEAD_DIM:HEAD_DIM + 1] + sink_e[:, g * gw:(g + 1) * gw]
        o = oa[:HEAD_DIM] * (1.0 / den)
        for r in range(0, KV_REP, 2):
            pair = jnp.concatenate([o[:, r * t:(r + 1) * t], o[:, (r + 1) * t:(r + 2) * t]], axis=0)
            lo = (g * KV_REP + r) * HEAD_DIM
            o_ref[0, :, lo:lo + 2 * HEAD_DIM] = pair.T.astype(BF16)


def _attention(q, k, vt, sink, lt):
    b, lp, attn_w = q.shape
    kv_w = k.shape[2]
    t = ATTN_TILE
    nt = lp // t
    cur = lambda bi, a: (bi, a, 0)
    prev = lambda bi, a: (bi, jnp.maximum(a - 1, 0), 0)
    nxt = lambda bi, a: (bi, jnp.minimum(a + 1, nt - 1), 0)
    first = lambda bi, a: (bi, 0, 0)
    kspec = lambda m: pl.BlockSpec((1, t, kv_w), m)
    vspec = lambda m: pl.BlockSpec((kv_w, t), lambda bi, a: (0, bi * nt + m(bi, a)[1]))
    return pl.pallas_call(
        functools.partial(_attn_kernel, lt=lt, n_heads=attn_w // HEAD_DIM),
        grid=(b, nt),
        in_specs=[pl.BlockSpec(memory_space=pltpu.SMEM), pl.BlockSpec((1, t, attn_w), cur),
                  kspec(prev), kspec(cur), kspec(nxt), kspec(first),
                  vspec(prev), vspec(cur), vspec(nxt), vspec(first)],
        out_specs=pl.BlockSpec((1, t, attn_w), cur),
        out_shape=jax.ShapeDtypeStruct((b, lp, attn_w), BF16),
        compiler_params=_params("arbitrary", "arbitrary"),
        name="attention",
    )(sink, q, k, k, k, k, vt, vt, vt, vt)


def _ssm_prep(a_re, a_im, log_dt, b_re, b_im, c_re, c_im, d, nsteps):
    hp = lax.Precision.HIGHEST
    t = CHUNK
    a_re, a_im, b_re, b_im = (v.astype(F32) for v in (a_re, a_im, b_re, b_im))
    n_l, _, n_g, n_p, n_h = b_re.shape
    dt = jnp.exp(log_dt.astype(F32))[..., None]

    def cpow(j):
        j = j.astype(F32)
        mag = jnp.exp((a_re * dt)[..., None] * j)
        ang = (a_im * dt)[..., None] * j
        return mag * jnp.cos(ang), mag * jnp.sin(ang)

    mag = jnp.exp(a_re * dt)
    ang = a_im * dt
    abr, abi = mag * jnp.cos(ang), mag * jnp.sin(ang)
    den = a_re * a_re + a_im * a_im
    nr = abr - 1.0
    cr = (nr * a_re + abi * a_im) / den
    ci = (abi * a_re - nr * a_im) / den
    bbr = cr[..., None] * b_re - ci[..., None] * b_im
    bbi = cr[..., None] * b_im + ci[..., None] * b_re

    pr0, pi0 = cpow(jnp.arange(t + 1))
    pr, pi = (jnp.repeat(v, n_h, axis=-1) for v in (pr0, pi0))
    c_ret = jnp.tile(jnp.swapaxes(c_re.astype(F32), -1, -2), (1, 1, 1, 1, t + 1))
    c_imt = jnp.tile(jnp.swapaxes(c_im.astype(F32), -1, -2), (1, 1, 1, 1, t + 1))
    car = c_ret * pr - c_imt * pi
    cai = c_ret * pi + c_imt * pr
    w = t * n_h
    kj = (jnp.einsum('ldgpx,ldgpk->ldgkx', car[..., :w], bbr, precision=hp)
          - jnp.einsum('ldgpx,ldgpk->ldgkx', cai[..., :w], bbi, precision=hp))
    kf, kb = kj[:, 0], kj[:, 1]
    dmat = jnp.eye(n_h, dtype=F32) * d.astype(F32).reshape(n_l, n_g, 1, n_h)
    centre = kf[..., :n_h] + kb[..., :n_h] + dmat
    kb_rev = jnp.flip(kb.reshape(n_l, n_g, n_h, t, n_h), axis=3).reshape(n_l, n_g, n_h, w)
    kall = jnp.concatenate([kb_rev[..., :w - n_h], centre, kf[..., n_h:]], axis=-1)
    mt = jnp.stack([kall[..., (t - 1 - s) * n_h:(2 * t - 1 - s) * n_h] for s in range(t)], axis=2)
    mt = mt.reshape(n_l, n_g, w, w)

    def b_pow(pw_r, pw_i, d_idx):
        pw_r, pw_i = (jnp.swapaxes(v, -1, -2)[..., None, :] for v in (pw_r, pw_i))
        br, bi = (jnp.swapaxes(v[:, d_idx], -1, -2)[:, :, None] for v in (bbr, bbi))
        shape = (n_l, n_g, t * n_h, n_p)
        return (pw_r * br - pw_i * bi).reshape(shape), (pw_r * bi + pw_i * br).reshape(shape)

    wf_r, wf_i = b_pow(pr0[:, 0, ..., t - 1::-1], pi0[:, 0, ..., t - 1::-1], 0)
    wb_r, wb_i = b_pow(pr0[:, 1, ..., :t], pi0[:, 1, ..., :t], 1)
    n_pair = n_g // 2

    def pair_cols(v):
        v = v.reshape(n_l, n_pair, 2, w, n_p)
        z = jnp.zeros_like(v[:, :, 0])
        return jnp.concatenate([jnp.concatenate([v[:, :, 0], z], axis=-1),
                                jnp.concatenate([z, v[:, :, 1]], axis=-1)], axis=-2)

    ws = jnp.concatenate([pair_cols(v) for v in (wf_r, wf_i, wb_r, wb_i)], axis=-1)

    fwd_rows = lambda v: v[:, 0, ..., n_h:]
    bwd_rows = lambda v: jnp.flip(v[:, 1].reshape(n_l, n_g, n_p, t + 1, n_h), axis=3)[..., :t, :].reshape(
        n_l, n_g, n_p, w)
    even = (jnp.arange(n_g) % 2 == 0)[None, :, None, None]

    def pair_rows(v):
        z = jnp.zeros_like(v)
        return jnp.where(even, jnp.concatenate([v, z], axis=2), jnp.concatenate([z, v], axis=2))

    wo = jnp.concatenate([pair_rows(v) for v in (fwd_rows(car), -fwd_rows(cai), bwd_rows(car), -bwd_rows(cai))],
                         axis=2)

    sr, si = cpow(t * (2 ** jnp.arange(nsteps)))
    lanes = lambda v: v.reshape(n_l, n_pair, 2 * n_p, nsteps)
    pw = jnp.stack([lanes(sr[:, 0]), lanes(si[:, 0]), lanes(sr[:, 1]), lanes(si[:, 1])], axis=-1)
    pw = jnp.transpose(pw, (0, 1, 3, 4, 2)).reshape(n_l, n_pair, 4 * nsteps, 2 * n_p)
    return mt.astype(BF16), ws.astype(BF16), wo.astype(BF16), pw


def _ssm_kernel(x_ref, mt_ref, ws_ref, wo_ref, pw_ref, y_ref, fr, fi, br, bi, *, nc, nsteps, pad):
    xs = [x_ref[i].astype(BF16) for i in range(2)]
    kw = xs[0].shape[1]
    s = (jnp.dot(xs[0], ws_ref[0, :kw], preferred_element_type=F32)
         + jnp.dot(xs[1], ws_ref[0, kw:], preferred_element_type=F32))
    hw = s.shape[1] // 4
    zeros = jnp.zeros((pad, hw), F32)
    for slot in range(2):
        fr[slot, 0:pad, :] = zeros
        fi[slot, 0:pad, :] = zeros
        br[slot, nc:nc + pad, :] = zeros
        bi[slot, nc:nc + pad, :] = zeros
    fr[0, pad:pad + nc, :] = s[:, 0:hw]
    fi[0, pad:pad + nc, :] = s[:, hw:2 * hw]
    br[0, 0:nc, :] = s[:, 2 * hw:3 * hw]
    bi[0, 0:nc, :] = s[:, 3 * hw:]
    rows = [(r0, min(SCAN_ROWS, nc - r0)) for r0 in range(0, nc, SCAN_ROWS)]

    d = 1
    for k in range(nsteps):
        src, dst = k % 2, (k + 1) % 2
        far, fai, bar, bai = (pw_ref[0, 4 * k + i:4 * k + i + 1, :] for i in range(4))
        for r0, n in rows:
            lo = pad + r0
            sr, si = fr[src, lo - d:lo - d + n, :], fi[src, lo - d:lo - d + n, :]
            fr[dst, lo:lo + n, :] = fr[src, lo:lo + n, :] + far * sr - fai * si
            fi[dst, lo:lo + n, :] = fi[src, lo:lo + n, :] + far * si + fai * sr
            sr, si = br[src, r0 + d:r0 + d + n, :], bi[src, r0 + d:r0 + d + n, :]
            br[dst, r0:r0 + n, :] = br[src, r0:r0 + n, :] + bar * sr - bai * si
            bi[dst, r0:r0 + n, :] = bi[src, r0:r0 + n, :] + bar * si + bai * sr
        d *= 2
    last = nsteps % 2
    hin = jnp.concatenate([fr[last, pad - 1:pad - 1 + nc, :], fi[last, pad - 1:pad - 1 + nc, :],
                           br[last, 1:1 + nc, :], bi[last, 1:1 + nc, :]], axis=1).astype(BF16)
    for i in range(2):
        y_ref[i] = (jnp.dot(xs[i], mt_ref[i], preferred_element_type=F32)
                    + jnp.dot(hin, wo_ref[i], preferred_element_type=F32))


def _ssm(xg, mt, ws, wo, pw, nc):
    n_g, rows, kw = xg.shape
    nsteps = max(1, math.ceil(math.log2(nc)))
    pad = max(8, 2 ** (nsteps - 1))
    pair = lambda bi, g: (g, 0, 0)
    io = lambda bi, g: (g, bi, 0)
    hw = ws.shape[2] // 4
    plane = pltpu.VMEM((2, nc + pad, hw), F32)
    return pl.pallas_call(
        functools.partial(_ssm_kernel, nc=nc, nsteps=nsteps, pad=pad),
        grid=(rows // nc, n_g // 2),
        in_specs=[pl.BlockSpec((2, nc, kw), io), pl.BlockSpec((2,) + mt.shape[1:], pair),
                  pl.BlockSpec((1,) + ws.shape[1:], pair), pl.BlockSpec((2,) + wo.shape[1:], pair),
                  pl.BlockSpec((1,) + pw.shape[1:], pair)],
        out_specs=pl.BlockSpec((2, nc, kw), io),
        out_shape=jax.ShapeDtypeStruct(xg.shape, F32),
        scratch_shapes=[plane, plane, plane, plane],
        compiler_params=_params("arbitrary", "arbitrary"),
        name="ssm",
    )(xg, mt, ws, wo, pw)


def _gelu_tanh(x):
    return 0.5 * x * (1.0 + jnp.tanh(math.sqrt(2.0 / math.pi) * (x + 0.044715 * (x * x * x))))


def _mix_kernel(x_ref, a_ref, yg_ref, wglu_ref, bglu_ref, ga_ref, gs_ref, woa_ref, wos_ref, o_ref, ybuf):
    an = _rms(a_ref[...].astype(F32), ga_ref[...]).astype(BF16)
    o_ref[...] = x_ref[...] + jnp.dot(an, woa_ref[...], preferred_element_type=F32)
    nch = yg_ref.shape[1]
    per = 2 * BLOCKS_PER_VREG
    for gb in range(ybuf.shape[0] // 2):
        grp = lambda i, half: yg_ref[gb * per + i, :, half * LANES:(half + 1) * LANES]
        a = _block_transpose([grp(i, 0) for i in range(BLOCKS_PER_VREG)])
        b = _block_transpose([grp(i, 1) for i in range(BLOCKS_PER_VREG)])
        c = _block_transpose([grp(BLOCKS_PER_VREG + i, 0) for i in range(BLOCKS_PER_VREG)])
        d = _block_transpose([grp(BLOCKS_PER_VREG + i, 1) for i in range(BLOCKS_PER_VREG)])
        for s in range(BLOCKS_PER_VREG):
            ybuf[2 * gb, pl.ds(s, nch, stride=CHUNK), :] = a[s]
            ybuf[2 * gb, pl.ds(BLOCKS_PER_VREG + s, nch, stride=CHUNK), :] = b[s]
            ybuf[2 * gb + 1, pl.ds(s, nch, stride=CHUNK), :] = c[s]
            ybuf[2 * gb + 1, pl.ds(BLOCKS_PER_VREG + s, nch, stride=CHUNK), :] = d[s]
    yg = jnp.concatenate([_gelu_tanh(ybuf[j]) for j in range(ybuf.shape[0])], axis=1)
    gate = jax.nn.sigmoid(jnp.dot(yg.astype(BF16), wglu_ref[...], preferred_element_type=F32) + bglu_ref[...])
    sn = _rms(yg * gate, gs_ref[...]).astype(BF16)
    o_ref[...] += jnp.dot(sn, wos_ref[...], preferred_element_type=F32)


def _mix(x, a, yg, wglu, bglu, ga, gs, woa, wos):
    n, d = x.shape
    tq = _tile(n, MIX_TILES)
    n_g, _, kw = yg.shape
    row = lambda i: (i, 0)
    return pl.pallas_call(
        _mix_kernel,
        grid=(n // tq,),
        in_specs=[pl.BlockSpec((tq, d), row), pl.BlockSpec((tq, a.shape[1]), row),
                  pl.BlockSpec((n_g, tq // CHUNK, kw), lambda i: (0, i, 0)),
                  _resident(wglu), _resident(bglu), _resident(ga), _resident(gs), _resident(woa), _resident(wos)],
        out_specs=pl.BlockSpec((tq, d), row),
        out_shape=jax.ShapeDtypeStruct((n, d), F32),
        scratch_shapes=[pltpu.VMEM((n_g * SSM_GROUP // LANES, tq, LANES), F32)],
        input_output_aliases={0: 0},
        compiler_params=_params("arbitrary"),
        name="mix",
    )(x, a, yg, wglu, bglu, ga, gs, woa, wos)


def _mlp_kernel(x_ref, g_ref, w1_ref, w2_ref, o_ref, h_ref):
    @pl.when(pl.program_id(1) == 0)
    def _():
        x = x_ref[...]
        h_ref[...] = _rms(x, g_ref[...]).astype(BF16)
        o_ref[...] = x

    t = jnp.dot(h_ref[...], w1_ref[...], preferred_element_type=F32)
    t = jnp.square(jnp.maximum(t, 0.0)).astype(BF16)
    o_ref[...] += jnp.dot(t, w2_ref[...], preferred_element_type=F32)


def _mlp(x, g, w1, w2, tf=1024):
    n, d = x.shape
    dff = w1.shape[1]
    tq = _tile(n, DENSE_TILES)
    row = lambda i, j: (i, 0)
    return pl.pallas_call(
        _mlp_kernel,
        grid=(n // tq, dff // tf),
        in_specs=[pl.BlockSpec((tq, d), row), _resident(g),
                  pl.BlockSpec((d, tf), lambda i, j: (0, j)), pl.BlockSpec((tf, d), lambda i, j: (j, 0))],
        out_specs=pl.BlockSpec((tq, d), row),
        out_shape=jax.ShapeDtypeStruct((n, d), F32),
        scratch_shapes=[pltpu.VMEM((tq, d), BF16)],
        input_output_aliases={0: 0},
        compiler_params=_params("arbitrary", "arbitrary"),
        name="mlp",
    )(x, g, w1, w2)


def _final_kernel(x_ref, g_ref, o_ref):
    o_ref[0] = _rms(x_ref[0], g_ref[...])


def _final_norm(x, g, s):
    b, _, d = x.shape
    tq = _tile(s, FINAL_TILES)
    return pl.pallas_call(
        _final_kernel,
        grid=(b, s // tq),
        in_specs=[pl.BlockSpec((pl.Element(1), pl.Element(tq), pl.Element(d)),
                               lambda bi, i: (bi, pl.multiple_of(N_META + i * tq, N_META), 0)), _resident(g)],
        out_specs=pl.BlockSpec((1, tq, d), lambda bi, i: (bi, i, 0)),
        out_shape=jax.ShapeDtypeStruct((b, s, d), F32),
        compiler_params=_params("arbitrary", "arbitrary"),
        name="final_norm",
    )(x, g)


def _rope_table(b, lt, lp):
    half = ROT_DIM // 2
    pos = jnp.arange(lp, dtype=F32)
    inv = ROPE_THETA ** (-jnp.arange(half, dtype=F32) * 2.0 / ROT_DIM)
    ang = pos[:, None] * inv[None, :]
    cos, sin = jnp.cos(ang), jnp.sin(ang)
    rest = HEAD_DIM - ROT_DIM
    c = jnp.concatenate([cos, cos, jnp.ones((lp, rest), F32)], axis=1)
    s1 = jnp.concatenate([-sin, jnp.zeros((lp, HEAD_DIM - half), F32)], axis=1)
    s2 = jnp.concatenate([jnp.zeros((lp, half), F32), sin, jnp.zeros((lp, rest), F32)], axis=1)
    valid = (jnp.arange(lp) < lt).astype(F32)[:, None]
    rep = LANES // HEAD_DIM
    tab = jnp.concatenate([jnp.tile(v * valid, (1, rep)) for v in (c, s1, s2)]
                          + [jnp.broadcast_to(valid, (lp, LANES))], axis=1)
    return jnp.tile(tab, (b, 1))


def _trunk(x_in, layers, meta_tokens, final_g):
    b, s, d = x_in.shape
    lt = N_META + s
    lp = _round_up(lt, ATTN_TILE)
    nc = lp // CHUNK
    meta = jnp.broadcast_to(meta_tokens[None].astype(F32), (b, N_META, d))
    x = jnp.concatenate([meta, x_in, jnp.zeros((b, lp - lt, d), F32)], axis=1).reshape(b * lp, d)
    tab = _rope_table(b, lt, lp)
    for p in layers:
        attn_w, kv_w = p["attn_w"], p["kv_w"]
        q, k, vt, xg = _inproj(x, p["g1"], p["w_in"], p["w_vt"], tab, attn_w, kv_w)
        a = _attention(q.reshape(b, lp, attn_w), k.reshape(b, lp, kv_w), vt, p["sink"], lt)
        yg = _ssm(xg, *p["ssm"], nc)
        x = _mix(x, a.reshape(b * lp, attn_w), yg, p["w_glu"], p["b_glu"], p["g_attn"], p["g_ssm"],
                 p["w_out_a"], p["w_out_s"])
        x = _mlp(x, p["g2"], p["w1"], p["w2"])
    return _final_norm(x.reshape(b, lp, d), final_g, s)


def kernel(x_prompt, x_sample, meta_tokens, norm1_g, w_in, attn_sink, ssm_a_re, ssm_a_im, ssm_log_dt, ssm_b_re, ssm_b_im, ssm_c_re, ssm_c_im, ssm_d, w_glu, b_glu, norm_attn_out, norm_ssm_out, w_out, norm2_g, w_mlp_in, w_mlp_out, final_norm_g):
    depth, d, in_w = w_in.shape
    attn_w = norm_attn_out.shape[1]
    ssm_w = norm_ssm_out.shape[1]
    kv_w = (in_w - attn_w - ssm_w) // 2
    nc_max = max(_round_up(N_META + xs.shape[1], ATTN_TILE) // CHUNK for xs in (x_prompt, x_sample))
    ssm = _ssm_prep(ssm_a_re, ssm_a_im, ssm_log_dt, ssm_b_re, ssm_b_im, ssm_c_re, ssm_c_im, ssm_d,
                    max(1, math.ceil(math.log2(nc_max))))
    row = lambda v: v.astype(F32)[None]
    layers = []
    for l in range(depth):
        layers.append(dict(
            attn_w=attn_w, kv_w=kv_w, ssm=tuple(v[l] for v in ssm),
            g1=row(norm1_g[l]), w_in=w_in[l].astype(BF16), sink=attn_sink[l].astype(F32),
            w_vt=w_in[l, :, attn_w + kv_w:attn_w + 2 * kv_w].T.astype(BF16),
            w_glu=w_glu[l].astype(BF16), b_glu=row(b_glu[l]),
            g_attn=row(norm_attn_out[l]), g_ssm=row(norm_ssm_out[l]),
            w_out_a=w_out[l, :attn_w].astype(BF16), w_out_s=w_out[l, attn_w:].astype(BF16),
            g2=row(norm2_g[l]), w1=w_mlp_in[l].astype(BF16), w2=w_mlp_out[l].astype(BF16)))
    fg = row(final_norm_g)
    return (_trunk(x_prompt, layers, meta_tokens, fg), _trunk(x_sample, layers, meta_tokens, fg))
```

```python
import functools
import math

import jax
import jax.numpy as jnp
from jax import lax
from jax.experimental import pallas as pl
from jax.experimental.pallas import tpu as pltpu

F32 = jnp.float32
BF16 = jnp.bfloat16

LANES = 128
BF16_ROWS = 16
N_META = 16
HEAD_DIM = 64
KV_REP = 4
ROT_DIM = HEAD_DIM // 4
ROPE_THETA = 500000.0
WINDOW = 128
ATTN_TILE = 128
SSM_GROUP = 16
CHUNK = 16
BLOCKS_PER_VREG = LANES // SSM_GROUP
NORM_EPS = 1e-6
NEG_INF = -1e30
LOG2E = math.log2(math.e)
VMEM_LIMIT = 56 * 1024 * 1024
DENSE_TILES = (768, 384, 256, 128)
MIX_TILES = (384, 256, 128)
FINAL_TILES = (1024, 512, 256, 128)
N_CHUNK_COLS = 512
SCAN_ROWS = 128


def _round_up(n, m):
    return (n + m - 1) // m * m


def _tile(n, candidates):
    for t in candidates:
        if n % t == 0:
            return t
    raise ValueError(f"{n} rows cannot be tiled by any of {candidates}")


def _rms(x, g):
    return x * lax.rsqrt(jnp.mean(x * x, axis=-1, keepdims=True) + NORM_EPS) * g


def _params(*sem):
    return pltpu.CompilerParams(dimension_semantics=sem, vmem_limit_bytes=VMEM_LIMIT)


def _resident(arr):
    nd = arr.ndim
    return pl.BlockSpec(arr.shape, lambda *_: (0,) * nd, pipeline_mode=pl.Buffered(1))


def _block_transpose(parts):
    blk = lax.broadcasted_iota(jnp.int32, parts[0].shape, 1) // SSM_GROUP
    b = BLOCKS_PER_VREG // 2
    while b:
        upper = (blk & b) != 0
        nxt = list(parts)
        for i in range(len(parts)):
            if i & b == 0:
                lo, hi = parts[i], parts[i + b]
                nxt[i] = jnp.where(upper, pltpu.roll(hi, SSM_GROUP * b, 1), lo)
                nxt[i + b] = jnp.where(upper, hi, pltpu.roll(lo, LANES - SSM_GROUP * b, 1))
        parts = nxt
        b //= 2
    return parts


def _inproj_kernel(x_ref, g_ref, w_ref, wvt_ref, tab_ref, q_ref, k_ref, vt_ref, xg_ref, ubuf, *, attn_w, kv_w):
    tq, d = x_ref.shape
    cos, s1, s2 = tab_ref[:, 0:LANES], tab_ref[:, LANES:2 * LANES], tab_ref[:, 2 * LANES:3 * LANES]
    valid = tab_ref[:, 3 * LANES:4 * LANES] > 0.0
    h = jnp.where(jnp.concatenate([valid] * (d // LANES), axis=1), _rms(x_ref[...], g_ref[...]), 0.0).astype(BF16)
    half = ROT_DIM // 2
    scale = HEAD_DIM ** -0.5 * LOG2E

    def rope(blk):
        return blk * cos + pltpu.roll(blk, LANES - half, 1) * s1 + pltpu.roll(blk, half, 1) * s2

    u0 = attn_w + 2 * kv_w
    spans = [(c, min(N_CHUNK_COLS, w_ref.shape[1] - c)) for c in range(u0, w_ref.shape[1], N_CHUNK_COLS)]
    spans += [(c, min(N_CHUNK_COLS, attn_w + kv_w - c)) for c in range(0, attn_w + kv_w, N_CHUNK_COLS)]
    for c0, width in spans:
        z = jnp.dot(h, w_ref[:, c0:c0 + width], preferred_element_type=F32)
        for j in range(width // LANES):
            col = c0 + j * LANES
            blk = z[:, j * LANES:(j + 1) * LANES]
            if col < attn_w:
                q_ref[:, col:col + LANES] = (rope(blk) * scale).astype(BF16)
            elif col < attn_w + kv_w:
                k_ref[:, col - attn_w:col - attn_w + LANES] = rope(blk).astype(BF16)
            else:
                ubuf[(col - u0) // LANES] = blk
    vt_ref[...] = lax.dot_general(wvt_ref[...], h, (((1,), (1,)), ((), ())),
                                  preferred_element_type=F32).astype(BF16)

    nch = tq // CHUNK
    per = 2 * BLOCKS_PER_VREG
    for gb in range(ubuf.shape[0] // 2):
        lo = [ubuf[2 * gb, pl.ds(s, nch, stride=CHUNK), :] for s in range(CHUNK)]
        hi = [ubuf[2 * gb + 1, pl.ds(s, nch, stride=CHUNK), :] for s in range(CHUNK)]
        a, b = _block_transpose(lo[:BLOCKS_PER_VREG]), _block_transpose(lo[BLOCKS_PER_VREG:])
        c, d = _block_transpose(hi[:BLOCKS_PER_VREG]), _block_transpose(hi[BLOCKS_PER_VREG:])
        for i in range(BLOCKS_PER_VREG):
            xg_ref[gb * per + i, :, 0:LANES] = a[i]
            xg_ref[gb * per + i, :, LANES:2 * LANES] = b[i]
            xg_ref[gb * per + BLOCKS_PER_VREG + i, :, 0:LANES] = c[i]
            xg_ref[gb * per + BLOCKS_PER_VREG + i, :, LANES:2 * LANES] = d[i]


def _inproj(x, g, w, wvt, tab, attn_w, kv_w):
    n, d = x.shape
    tq = _tile(n, DENSE_TILES)
    ssm_w = w.shape[1] - attn_w - 2 * kv_w
    n_g = ssm_w // SSM_GROUP
    row = lambda i: (i, 0)
    return pl.pallas_call(
        functools.partial(_inproj_kernel, attn_w=attn_w, kv_w=kv_w),
        grid=(n // tq,),
        in_specs=[pl.BlockSpec((tq, d), row), _resident(g), _resident(w), _resident(wvt),
                  pl.BlockSpec((tq, 4 * LANES), row)],
        out_specs=[pl.BlockSpec((tq, attn_w), row), pl.BlockSpec((tq, kv_w), row),
                   pl.BlockSpec((kv_w, tq), lambda i: (0, i)),
                   pl.BlockSpec((n_g, tq // CHUNK, CHUNK * SSM_GROUP), lambda i: (0, i, 0))],
        out_shape=[jax.ShapeDtypeStruct((n, attn_w), BF16), jax.ShapeDtypeStruct((n, kv_w), BF16),
                   jax.ShapeDtypeStruct((kv_w, n), BF16),
                   jax.ShapeDtypeStruct((n_g, n // CHUNK, CHUNK * SSM_GROUP), F32)],
        scratch_shapes=[pltpu.VMEM((ssm_w // LANES, tq, LANES), F32)],
        compiler_params=_params("arbitrary"),
        name="inproj",
    )(x, g, w, wvt, tab)


def _attn_kernel(sink_ref, q_ref, kp_ref, kc_ref, kn_ref, k0_ref, vp_ref, vc_ref, vn_ref, v0_ref, o_ref, *,
                 lt, n_heads):
    a = pl.program_id(1)
    t = ATTN_TILE
    q = q_ref[0]
    kall = jnp.concatenate([kp_ref[0], kc_ref[0], kn_ref[0], k0_ref[0]], axis=0)
    vtall = jnp.concatenate([vp_ref[...], vc_ref[...], vn_ref[...], v0_ref[...]], axis=1)
    qpos = a * t + lax.broadcasted_iota(jnp.int32, (3 * t, t), 1)
    kpos = (a - 1) * t + lax.broadcasted_iota(jnp.int32, (3 * t, t), 0)
    band = (jnp.abs(qpos - kpos) <= WINDOW) & (kpos >= 0) & (kpos < lt)
    mpos = lax.broadcasted_iota(jnp.int32, (t, t), 0)
    far_meta = (mpos < N_META) & (a * t + lax.broadcasted_iota(jnp.int32, (t, t), 1) - mpos > WINDOW)
    bias = jnp.where(jnp.concatenate([band, far_meta], axis=0), 0.0, NEG_INF)
    bias = jnp.concatenate([bias] * n_heads, axis=1)
    nt = (((1,), (1,)), ((), ()))
    n_groups = n_heads // KV_REP
    zeros = jnp.zeros((t, HEAD_DIM), BF16)

    def padded(h):
        g = h // KV_REP
        return jnp.concatenate([zeros] * g + [q[:, h * HEAD_DIM:(h + 1) * HEAD_DIM]] + [zeros] * (n_groups - 1 - g),
                               axis=1)

    qq = jnp.concatenate([padded(h) for h in range(n_heads)], axis=0)
    s = lax.dot_general(kall, qq, nt, preferred_element_type=F32) + bias
    sink = jnp.concatenate([jnp.full((1, t), sink_ref[h] * LOG2E, F32) for h in range(n_heads)], axis=1)
    m = jnp.maximum(jnp.max(s, axis=0, keepdims=True), sink)
    e = jnp.exp2(s - m).astype(BF16)
    sink_e = jnp.exp2(sink - m)
    ones = jnp.ones((BF16_ROWS, 4 * t), BF16)
    gw = KV_REP * t
    for g in range(n_groups):
        ks = slice(g * HEAD_DIM, (g + 1) * HEAD_DIM)
        oa = jnp.dot(jnp.concatenate([vtall[ks, :], ones], axis=0), e[:, g * gw:(g + 1) * gw],
                     preferred_element_type=F32)
        den = oa[HEAD_DIM:HEAD_DIM + 1] + sink_e[:, g * gw:(g + 1) * gw]
        o = oa[:HEAD_DIM] * (1.0 / den)
        for r in range(0, KV_REP, 2):
            pair = jnp.concatenate([o[:, r * t:(r + 1) * t], o[:, (r + 1) * t:(r + 2) * t]], axis=0)
            lo = (g * KV_REP + r) * HEAD_DIM
            o_ref[0, :, lo:lo + 2 * HEAD_DIM] = pair.T.astype(BF16)


def _attention(q, k, vt, sink, lt):
    b, lp, attn_w = q.shape
    kv_w = k.shape[2]
    t = ATTN_TILE
    nt = lp // t
    cur = lambda bi, a: (bi, a, 0)
    prev = lambda bi, a: (bi, jnp.maximum(a - 1, 0), 0)
    nxt = lambda bi, a: (bi, jnp.minimum(a + 1, nt - 1), 0)
    first = lambda bi, a: (bi, 0, 0)
    kspec = lambda m: pl.BlockSpec((1, t, kv_w), m)
    vspec = lambda m: pl.BlockSpec((kv_w, t), lambda bi, a: (0, bi * nt + m(bi, a)[1]))
    return pl.pallas_call(
        functools.partial(_attn_kernel, lt=lt, n_heads=attn_w // HEAD_DIM),
        grid=(b, nt),
        in_specs=[pl.BlockSpec(memory_space=pltpu.SMEM), pl.BlockSpec((1, t, attn_w), cur),
                  kspec(prev), kspec(cur), kspec(nxt), kspec(first),
                  vspec(prev), vspec(cur), vspec(nxt), vspec(first)],
        out_specs=pl.BlockSpec((1, t, attn_w), cur),
        out_shape=jax.ShapeDtypeStruct((b, lp, attn_w), BF16),
        compiler_params=_params("arbitrary", "arbitrary"),
        name="attention",
    )(sink, q, k, k, k, k, vt, vt, vt, vt)


def _ssm_prep(a_re, a_im, log_dt, b_re, b_im, c_re, c_im, d, nsteps):
    hp = lax.Precision.HIGHEST
    t = CHUNK
    a_re, a_im, b_re, b_im = (v.astype(F32) for v in (a_re, a_im, b_re, b_im))
    n_l, _, n_g, n_p, n_h = b_re.shape
    dt = jnp.exp(log_dt.astype(F32))[..., None]

    def cpow(j):
        j = j.astype(F32)
        mag = jnp.exp((a_re * dt)[..., None] * j)
        ang = (a_im * dt)[..., None] * j
        return mag * jnp.cos(ang), mag * jnp.sin(ang)

    mag = jnp.exp(a_re * dt)
    ang = a_im * dt
    abr, abi = mag * jnp.cos(ang), mag * jnp.sin(ang)
    den = a_re * a_re + a_im * a_im
    nr = abr - 1.0
    cr = (nr * a_re + abi * a_im) / den
    ci = (abi * a_re - nr * a_im) / den
    bbr = cr[..., None] * b_re - ci[..., None] * b_im
    bbi = cr[..., None] * b_im + ci[..., None] * b_re

    pr0, pi0 = cpow(jnp.arange(t + 1))
    pr, pi = (jnp.repeat(v, n_h, axis=-1) for v in (pr0, pi0))
    c_ret = jnp.tile(jnp.swapaxes(c_re.astype(F32), -1, -2), (1, 1, 1, 1, t + 1))
    c_imt = jnp.tile(jnp.swapaxes(c_im.astype(F32), -1, -2), (1, 1, 1, 1, t + 1))
    car = c_ret * pr - c_imt * pi
    cai = c_ret * pi + c_imt * pr
    w = t * n_h
    kj = (jnp.einsum('ldgpx,ldgpk->ldgkx', car[..., :w], bbr, precision=hp)
          - jnp.einsum('ldgpx,ldgpk->ldgkx', cai[..., :w], bbi, precision=hp))
    kf, kb = kj[:, 0], kj[:, 1]
    dmat = jnp.eye(n_h, dtype=F32) * d.astype(F32).reshape(n_l, n_g, 1, n_h)
    centre = kf[..., :n_h] + kb[..., :n_h] + dmat
    kb_rev = jnp.flip(kb.reshape(n_l, n_g, n_h, t, n_h), axis=3).reshape(n_l, n_g, n_h, w)
    kall = jnp.concatenate([kb_rev[..., :w - n_h], centre, kf[..., n_h:]], axis=-1)
    mt = jnp.stack([kall[..., (t - 1 - s) * n_h:(2 * t - 1 - s) * n_h] for s in range(t)], axis=2)
    mt = mt.reshape(n_l, n_g, w, w)

    def b_pow(pw_r, pw_i, d_idx):
        pw_r, pw_i = (jnp.swapaxes(v, -1, -2)[..., None, :] for v in (pw_r, pw_i))
        br, bi = (jnp.swapaxes(v[:, d_idx], -1, -2)[:, :, None] for v in (bbr, bbi))
        shape = (n_l, n_g, t * n_h, n_p)
        return (pw_r * br - pw_i * bi).reshape(shape), (pw_r * bi + pw_i * br).reshape(shape)

    wf_r, wf_i = b_pow(pr0[:, 0, ..., t - 1::-1], pi0[:, 0, ..., t - 1::-1], 0)
    wb_r, wb_i = b_pow(pr0[:, 1, ..., :t], pi0[:, 1, ..., :t], 1)
    n_pair = n_g // 2

    def pair_cols(v):
        v = v.reshape(n_l, n_pair, 2, w, n_p)
        z = jnp.zeros_like(v[:, :, 0])
        return jnp.concatenate([jnp.concatenate([v[:, :, 0], z], axis=-1),
                                jnp.concatenate([z, v[:, :, 1]], axis=-1)], axis=-2)

    ws = jnp.concatenate([pair_cols(v) for v in (wf_r, wf_i, wb_r, wb_i)], axis=-1)

    fwd_rows = lambda v: v[:, 0, ..., n_h:]
    bwd_rows = lambda v: jnp.flip(v[:, 1].reshape(n_l, n_g, n_p, t + 1, n_h), axis=3)[..., :t, :].reshape(
        n_l, n_g, n_p, w)
    even = (jnp.arange(n_g) % 2 == 0)[None, :, None, None]

    def pair_rows(v):
        z = jnp.zeros_like(v)
        return jnp.where(even, jnp.concatenate([v, z], axis=2), jnp.concatenate([z, v], axis=2))

    wo = jnp.concatenate([pair_rows(v) for v in (fwd_rows(car), -fwd_rows(cai), bwd_rows(car), -bwd_rows(cai))],
                         axis=2)

    sr, si = cpow(t * (2 ** jnp.arange(nsteps)))
    lanes = lambda v: v.reshape(n_l, n_pair, 2 * n_p, nsteps)
    pw = jnp.stack([lanes(sr[:, 0]), lanes(si[:, 0]), lanes(sr[:, 1]), lanes(si[:, 1])], axis=-1)
    pw = jnp.transpose(pw, (0, 1, 3, 4, 2)).reshape(n_l, n_pair, 4 * nsteps, 2 * n_p)
    return mt.astype(BF16), ws.astype(BF16), wo.astype(BF16), pw


def _ssm_kernel(x_ref, mt_ref, ws_ref, wo_ref, pw_ref, y_ref, fr, fi, br, bi, *, nc, nsteps, pad):
    xs = [x_ref[i].astype(BF16) for i in range(2)]
    kw = xs[0].shape[1]
    s = (jnp.dot(xs[0], ws_ref[0, :kw], preferred_element_type=F32)
         + jnp.dot(xs[1], ws_ref[0, kw:], preferred_element_type=F32))
    hw = s.shape[1] // 4
    zeros = jnp.zeros((pad, hw), F32)
    for slot in range(2):
        fr[slot, 0:pad, :] = zeros
        fi[slot, 0:pad, :] = zeros
        br[slot, nc:nc + pad, :] = zeros
        bi[slot, nc:nc + pad, :] = zeros
    fr[0, pad:pad + nc, :] = s[:, 0:hw]
    fi[0, pad:pad + nc, :] = s[:, hw:2 * hw]
    br[0, 0:nc, :] = s[:, 2 * hw:3 * hw]
    bi[0, 0:nc, :] = s[:, 3 * hw:]
    rows = [(r0, min(SCAN_ROWS, nc - r0)) for r0 in range(0, nc, SCAN_ROWS)]

    d = 1
    for k in range(nsteps):
        src, dst = k % 2, (k + 1) % 2
        far, fai, bar, bai = (pw_ref[0, 4 * k + i:4 * k + i + 1, :] for i in range(4))
        for r0, n in rows:
            lo = pad + r0
            sr, si = fr[src, lo - d:lo - d + n, :], fi[src, lo - d:lo - d + n, :]
            fr[dst, lo:lo + n, :] = fr[src, lo:lo + n, :] + far * sr - fai * si
            fi[dst, lo:lo + n, :] = fi[src, lo:lo + n, :] + far * si + fai * sr
            sr, si = br[src, r0 + d:r0 + d + n, :], bi[src, r0 + d:r0 + d + n, :]
            br[dst, r0:r0 + n, :] = br[src, r0:r0 + n, :] + bar * sr - bai * si
            bi[dst, r0:r0 + n, :] = bi[src, r0:r0 + n, :] + bar * si + bai * sr
        d *= 2
    last = nsteps % 2
    hin = jnp.concatenate([fr[last, pad - 1:pad - 1 + nc, :], fi[last, pad - 1:pad - 1 + nc, :],
                           br[last, 1:1 + nc, :], bi[last, 1:1 + nc, :]], axis=1).astype(BF16)
    for i in range(2):
        y_ref[i] = (jnp.dot(xs[i], mt_ref[i], preferred_element_type=F32)
                    + jnp.dot(hin, wo_ref[i], preferred_element_type=F32))


def _ssm(xg, mt, ws, wo, pw, nc, layer):
    n_g, rows, kw = xg.shape
    nsteps = max(1, math.ceil(math.log2(nc)))
    pad = max(8, 2 ** (nsteps - 1))
    pair = lambda bi, g: (layer, g, 0, 0)
    io = lambda bi, g: (g, bi, 0)
    hw = ws.shape[3] // 4
    plane = pltpu.VMEM((2, nc + pad, hw), F32)
    return pl.pallas_call(
        functools.partial(_ssm_kernel, nc=nc, nsteps=nsteps, pad=pad),
        grid=(rows // nc, n_g // 2),
        in_specs=[pl.BlockSpec((2, nc, kw), io), pl.BlockSpec((None, 2) + mt.shape[2:], pair),
                  pl.BlockSpec((None, 1) + ws.shape[2:], pair), pl.BlockSpec((None, 2) + wo.shape[2:], pair),
                  pl.BlockSpec((None, 1) + pw.shape[2:], pair)],
        out_specs=pl.BlockSpec((2, nc, kw), io),
        out_shape=jax.ShapeDtypeStruct(xg.shape, F32),
        scratch_shapes=[plane, plane, plane, plane],
        compiler_params=_params("arbitrary", "arbitrary"),
        name="ssm",
    )(xg, mt, ws, wo, pw)


def _gelu_tanh(x):
    return 0.5 * x * (1.0 + jnp.tanh(math.sqrt(2.0 / math.pi) * (x + 0.044715 * (x * x * x))))


def _mix_kernel(x_ref, a_ref, yg_ref, wglu_ref, bglu_ref, ga_ref, gs_ref, woa_ref, wos_ref, o_ref, ybuf):
    an = _rms(a_ref[...].astype(F32), ga_ref[...]).astype(BF16)
    o_ref[...] = x_ref[...] + jnp.dot(an, woa_ref[...], preferred_element_type=F32)
    nch = yg_ref.shape[1]
    per = 2 * BLOCKS_PER_VREG
    for gb in range(ybuf.shape[0] // 2):
        grp = lambda i, half: yg_ref[gb * per + i, :, half * LANES:(half + 1) * LANES]
        a = _block_transpose([grp(i, 0) for i in range(BLOCKS_PER_VREG)])
        b = _block_transpose([grp(i, 1) for i in range(BLOCKS_PER_VREG)])
        c = _block_transpose([grp(BLOCKS_PER_VREG + i, 0) for i in range(BLOCKS_PER_VREG)])
        d = _block_transpose([grp(BLOCKS_PER_VREG + i, 1) for i in range(BLOCKS_PER_VREG)])
        for s in range(BLOCKS_PER_VREG):
            ybuf[2 * gb, pl.ds(s, nch, stride=CHUNK), :] = a[s]
            ybuf[2 * gb, pl.ds(BLOCKS_PER_VREG + s, nch, stride=CHUNK), :] = b[s]
            ybuf[2 * gb + 1, pl.ds(s, nch, stride=CHUNK), :] = c[s]
            ybuf[2 * gb + 1, pl.ds(BLOCKS_PER_VREG + s, nch, stride=CHUNK), :] = d[s]
    yg = jnp.concatenate([_gelu_tanh(ybuf[j]) for j in range(ybuf.shape[0])], axis=1)
    gate = jax.nn.sigmoid(jnp.dot(yg.astype(BF16), wglu_ref[...], preferred_element_type=F32) + bglu_ref[...])
    sn = _rms(yg * gate, gs_ref[...]).astype(BF16)
    o_ref[...] += jnp.dot(sn, wos_ref[...], preferred_element_type=F32)


def _mix(x, a, yg, wglu, bglu, ga, gs, woa, wos):
    n, d = x.shape
    tq = _tile(n, MIX_TILES)
    n_g, _, kw = yg.shape
    row = lambda i: (i, 0)
    return pl.pallas_call(
        _mix_kernel,
        grid=(n // tq,),
        in_specs=[pl.BlockSpec((tq, d), row), pl.BlockSpec((tq, a.shape[1]), row),
                  pl.BlockSpec((n_g, tq // CHUNK, kw), lambda i: (0, i, 0)),
                  _resident(wglu), _resident(bglu), _resident(ga), _resident(gs), _resident(woa), _resident(wos)],
        out_specs=pl.BlockSpec((tq, d), row),
        out_shape=jax.ShapeDtypeStruct((n, d), F32),
        scratch_shapes=[pltpu.VMEM((n_g * SSM_GROUP // LANES, tq, LANES), F32)],
        input_output_aliases={0: 0},
        compiler_params=_params("arbitrary"),
        name="mix",
    )(x, a, yg, wglu, bglu, ga, gs, woa, wos)


def _mlp_kernel(x_ref, g_ref, w1_ref, w2_ref, o_ref, h_ref):
    @pl.when(pl.program_id(1) == 0)
    def _():
        x = x_ref[...]
        h_ref[...] = _rms(x, g_ref[...]).astype(BF16)
        o_ref[...] = x

    t = jnp.dot(h_ref[...], w1_ref[...], preferred_element_type=F32)
    t = jnp.square(jnp.maximum(t, 0.0)).astype(BF16)
    o_ref[...] += jnp.dot(t, w2_ref[...], preferred_element_type=F32)


def _mlp(x, g, w1, w2, tf=1024):
    n, d = x.shape
    dff = w1.shape[1]
    tq = _tile(n, DENSE_TILES)
    row = lambda i, j: (i, 0)
    return pl.pallas_call(
        _mlp_kernel,
        grid=(n // tq, dff // tf),
        in_specs=[pl.BlockSpec((tq, d), row), _resident(g),
                  pl.BlockSpec((d, tf), lambda i, j: (0, j)), pl.BlockSpec((tf, d), lambda i, j: (j, 0))],
        out_specs=pl.BlockSpec((tq, d), row),
        out_shape=jax.ShapeDtypeStruct((n, d), F32),
        scratch_shapes=[pltpu.VMEM((tq, d), BF16)],
        input_output_aliases={0: 0},
        compiler_params=_params("arbitrary", "arbitrary"),
        name="mlp",
    )(x, g, w1, w2)


def _final_kernel(x_ref, g_ref, o_ref):
    o_ref[0] = _rms(x_ref[0], g_ref[...])


def _final_norm(x, g, s):
    b, _, d = x.shape
    tq = _tile(s, FINAL_TILES)
    return pl.pallas_call(
        _final_kernel,
        grid=(b, s // tq),
        in_specs=[pl.BlockSpec((pl.Element(1), pl.Element(tq), pl.Element(d)),
                               lambda bi, i: (bi, pl.multiple_of(N_META + i * tq, N_META), 0)), _resident(g)],
        out_specs=pl.BlockSpec((1, tq, d), lambda bi, i: (bi, i, 0)),
        out_shape=jax.ShapeDtypeStruct((b, s, d), F32),
        compiler_params=_params("arbitrary", "arbitrary"),
        name="final_norm",
    )(x, g)


def _rope_table(b, lt, lp):
    half = ROT_DIM // 2
    pos = jnp.arange(lp, dtype=F32)
    inv = ROPE_THETA ** (-jnp.arange(half, dtype=F32) * 2.0 / ROT_DIM)
    ang = pos[:, None] * inv[None, :]
    cos, sin = jnp.cos(ang), jnp.sin(ang)
    rest = HEAD_DIM - ROT_DIM
    c = jnp.concatenate([cos, cos, jnp.ones((lp, rest), F32)], axis=1)
    s1 = jnp.concatenate([-sin, jnp.zeros((lp, HEAD_DIM - half), F32)], axis=1)
    s2 = jnp.concatenate([jnp.zeros((lp, half), F32), sin, jnp.zeros((lp, rest), F32)], axis=1)
    valid = (jnp.arange(lp) < lt).astype(F32)[:, None]
    rep = LANES // HEAD_DIM
    tab = jnp.concatenate([jnp.tile(v * valid, (1, rep)) for v in (c, s1, s2)]
                          + [jnp.broadcast_to(valid, (lp, LANES))], axis=1)
    return jnp.tile(tab, (b, 1))


def _trunk(x_in, layers, meta_tokens, final_g):
    b, s, d = x_in.shape
    lt = N_META + s
    lp = _round_up(lt, ATTN_TILE)
    nc = lp // CHUNK
    meta = jnp.broadcast_to(meta_tokens[None].astype(F32), (b, N_META, d))
    x = jnp.concatenate([meta, x_in, jnp.zeros((b, lp - lt, d), F32)], axis=1).reshape(b * lp, d)
    tab = _rope_table(b, lt, lp)
    for p in layers:
        attn_w, kv_w = p["attn_w"], p["kv_w"]
        q, k, vt, xg = _inproj(x, p["g1"], p["w_in"], p["w_vt"], tab, attn_w, kv_w)
        a = _attention(q.reshape(b, lp, attn_w), k.reshape(b, lp, kv_w), vt, p["sink"], lt)
        yg = _ssm(xg, *p["ssm"], nc, p["layer"])
        x = _mix(x, a.reshape(b * lp, attn_w), yg, p["w_glu"], p["b_glu"], p["g_attn"], p["g_ssm"],
                 p["w_out_a"], p["w_out_s"])
        x = _mlp(x, p["g2"], p["w1"], p["w2"])
    return _final_norm(x.reshape(b, lp, d), final_g, s)


def kernel(x_prompt, x_sample, meta_tokens, norm1_g, w_in, attn_sink, ssm_a_re, ssm_a_im, ssm_log_dt, ssm_b_re, ssm_b_im, ssm_c_re, ssm_c_im, ssm_d, w_glu, b_glu, norm_attn_out, norm_ssm_out, w_out, norm2_g, w_mlp_in, w_mlp_out, final_norm_g):
    depth, d, in_w = w_in.shape
    attn_w = norm_attn_out.shape[1]
    ssm_w = norm_ssm_out.shape[1]
    kv_w = (in_w - attn_w - ssm_w) // 2
    nc_max = max(_round_up(N_META + xs.shape[1], ATTN_TILE) // CHUNK for xs in (x_prompt, x_sample))
    ssm = _ssm_prep(ssm_a_re, ssm_a_im, ssm_log_dt, ssm_b_re, ssm_b_im, ssm_c_re, ssm_c_im, ssm_d,
                    max(1, math.ceil(math.log2(nc_max))))
    row = lambda v: v.astype(F32)[None]
    layers = []
    for l in range(depth):
        layers.append(dict(
            attn_w=attn_w, kv_w=kv_w, ssm=ssm, layer=l,
            g1=row(norm1_g[l]), w_in=w_in[l].astype(BF16), sink=attn_sink[l].astype(F32),
            w_vt=w_in[l, :, attn_w + kv_w:attn_w + 2 * kv_w].T.astype(BF16),
            w_glu=w_glu[l].astype(BF16), b_glu=row(b_glu[l]),
            g_attn=row(norm_attn_out[l]), g_ssm=row(norm_ssm_out[l]),
            w_out_a=w_out[l, :attn_w].astype(BF16), w_out_s=w_out[l, attn_w:].astype(BF16),
            g2=row(norm2_g[l]), w1=w_mlp_in[l].astype(BF16), w2=w_mlp_out[l].astype(BF16)))
    fg = row(final_norm_g)
    return (_trunk(x_prompt, layers, meta_tokens, fg), _trunk(x_sample, layers, meta_tokens, fg))
```
